```python
import math
import jax, jax.numpy as jnp
from jax import lax
import numpy as np

D_MODEL = 1024
BATCH = 4
SEQ = 4096
DEPTH = 4

GRID_W = 64
CTX_LEN = 256
N_MIXERS = 4
Q_BLOCK = 128
CHUNK = 64
ROPE_THETA = 10000.0
NORM_EPS = 1e-6
N_MOD = 9
D_FF = 2816

DIFF_HEADS = 8
DIFF_HEAD_DIM = D_MODEL // DIFF_HEADS // 2

RET_HEADS = 4
RET_KEY_DIM = D_MODEL // RET_HEADS
RET_VAL_DIM = 2 * D_MODEL // RET_HEADS

HGRN_EXPAND = 128
HGRN_HEADS = D_MODEL // HGRN_EXPAND
HGRN_KEY_DIM = HGRN_EXPAND
HGRN_VAL_DIM = D_MODEL // HGRN_HEADS

GQA_HEAD_DIM = 128
GQA_Q_HEADS = D_MODEL // GQA_HEAD_DIM
GQA_KV_HEADS = 2
GQA_GROUP = GQA_Q_HEADS // GQA_KV_HEADS

kernel_name = "hybrid_interleaved_diffusion_trunk"


def _rms(x, eps=NORM_EPS):
    xf = x.astype(jnp.float32)
    return (xf * lax.rsqrt(jnp.mean(xf * xf, axis=-1, keepdims=True) + eps)).astype(x.dtype)


def _modulate(h, shift, scale):
    return _rms(h) * (1.0 + scale) + shift


def _split_mod(v):
    v = v.reshape(*v.shape[:-1], N_MOD, 1, D_MODEL)
    return [v[..., k, :, :] for k in range(N_MOD)]


def _swiglu(h, w13, w2):
    a, b = jnp.split(h @ w13, 2, axis=-1)
    return (jax.nn.silu(a) * b) @ w2


def _axial_rope_tables(n_tokens, head_dim):
    rows = n_tokens // GRID_W
    row = jnp.repeat(jnp.arange(rows, dtype=jnp.float32), GRID_W)
    col = jnp.tile(jnp.arange(GRID_W, dtype=jnp.float32), rows)
    nq = head_dim // 4
    inv = ROPE_THETA ** (-jnp.arange(nq, dtype=jnp.float32) * 2.0 / (head_dim // 2))
    ang = jnp.stack([row[:, None] * inv, col[:, None] * inv], axis=1)
    return jnp.cos(ang), jnp.sin(ang)


def _apply_rope(x, cos, sin):
    shp = x.shape
    xr = x.reshape(*shp[:-1], 2, 2, shp[-1] // 4)
    x1, x2 = xr[..., 0, :], xr[..., 1, :]
    extra = len(shp) - 3
    c = cos.reshape(cos.shape[0], *([1] * extra), 2, -1)
    s = sin.reshape(sin.shape[0], *([1] * extra), 2, -1)
    out = jnp.stack([x1 * c - x2 * s, x2 * c + x1 * s], axis=-2)
    return out.reshape(shp).astype(x.dtype)


def _to_blocks(t):
    B, T = t.shape[:2]
    return jnp.moveaxis(t.reshape(B, T // Q_BLOCK, Q_BLOCK, *t.shape[2:]), 1, 0)


def _from_blocks(t):
    nb, B, qb = t.shape[:3]
    return jnp.moveaxis(t, 0, 1).reshape(B, nb * qb, *t.shape[3:])


def _chunk_gla(q, k, v, log_a, s0):
    B, T, H, _ = q.shape
    n = T // CHUNK

    def chunks(t):
        return jnp.moveaxis(t.astype(jnp.float32).reshape(B, n, CHUNK, *t.shape[2:]), 1, 0)

    causal = jnp.tril(jnp.ones((CHUNK, CHUNK), dtype=bool))

    def step(state, inp):
        qc, kc, vc, lc = inp
        b = jnp.cumsum(lc, axis=1)
        b_last = b[:, -1]
        q_dec = qc * jnp.exp(b)
        k_dec = kc * jnp.exp(-b)
        att = jnp.where(causal, jnp.einsum('bchk,bshk->bhcs', q_dec, k_dec), 0.0)
        o = (jnp.einsum('bchk,bhkv->bchv', q_dec, state)
             + jnp.einsum('bhcs,bshv->bchv', att, vc))
        state = (jnp.exp(b_last)[..., None] * state
                 + jnp.einsum('bshk,bshv->bhkv', kc * jnp.exp(b_last[:, None] - b), vc))
        return state, o

    state, o = lax.scan(step, s0, (chunks(q), chunks(k), chunks(v), chunks(log_a)))
    o = jnp.moveaxis(o, 0, 1).reshape(B, T, H, v.shape[-1]).astype(v.dtype)
    return o, state


def _bidir_scan(q_c, v_c, dirs_c, q_l, v_l, dirs_l):
    B, _, H, dk = q_c.shape
    s0 = jnp.zeros((B, H, dk, v_c.shape[-1]), jnp.float32)
    rev = lambda t: t[:, ::-1]
    (kcf, lcf), (kcb, lcb) = dirs_c
    (klf, llf), (klb, llb) = dirs_l
    oc_f, s_f = _chunk_gla(q_c, kcf, v_c, lcf, s0)
    oc_b, s_b = _chunk_gla(rev(q_c), rev(kcb), rev(v_c), rev(lcb), s0)
    ol_f, _ = _chunk_gla(q_l, klf, v_l, llf, s_f)
    ol_b, _ = _chunk_gla(rev(q_l), rev(klb), rev(v_l), rev(llb), s_b)
    return oc_f + rev(oc_b), ol_f + rev(ol_b)


def _diff_attention(h_lat, h_ctx, w_in, w_out, lam, subln_g, layer_idx, need_ctx):
    H, d = DIFF_HEADS, DIFF_HEAD_DIM
    lam_init = 0.8 - 0.6 * math.exp(-0.3 * layer_idx)
    l32 = lam.astype(jnp.float32)
    lam_full = jnp.exp(jnp.sum(l32[0] * l32[1])) - jnp.exp(jnp.sum(l32[2] * l32[3])) + lam_init
    cos, sin = _axial_rope_tables(h_lat.shape[1], d)

    def project(h, rotate):
        B, T, _ = h.shape
        q, k, v = jnp.split(h @ w_in, 3, axis=-1)
        q = q.reshape(B, T, H, 2, d) * (d ** -0.5)
        k = k.reshape(B, T, H, 2, d)
        if rotate:
            q, k = _apply_rope(q, cos, sin), _apply_rope(k, cos, sin)
        return q, k, v.reshape(B, T, H, 2 * d)

    def attend(q, k, v):
        s = jnp.einsum('bqhjd,bshjd->bhjqs', q, k).astype(jnp.float32)
        p = jax.nn.softmax(s, axis=-1)
        a = (p[:, :, 0] - lam_full * p[:, :, 1]).astype(v.dtype)
        o = jnp.einsum('bhqs,bshe->bqhe', a, v)
        return _rms(o) * subln_g * (1.0 - lam_init)

    def out_proj(o):
        B, T = o.shape[:2]
        return o.reshape(B, T, D_MODEL) @ w_out

    ql, kl, vl = project(h_lat, True)
    qc, kc, vc = project(h_ctx, False)
    k_all = jnp.concatenate([kc, kl], axis=1)
    v_all = jnp.concatenate([vc, vl], axis=1)
    o_lat = _from_blocks(lax.map(lambda qb: attend(qb, k_all, v_all), _to_blocks(ql)))
    o_ctx = out_proj(attend(qc, kc, vc)) if need_ctx else None
    return out_proj(o_lat), o_ctx


def _retention(h_lat, h_ctx, w_in, w_out, decay_exp, need_ctx):
    H, dk, dv = RET_HEADS, RET_KEY_DIM, RET_VAL_DIM
    D = D_MODEL
    log_gamma = jnp.log1p(-jnp.exp2(-decay_exp.astype(jnp.float32)))
    cos, sin = _axial_rope_tables(h_lat.shape[1], dk)

    def project(h, rotate):
        B, T, _ = h.shape
        q, k, v, g = jnp.split(h @ w_in, [D, 2 * D, 4 * D], axis=-1)
        q = q.reshape(B, T, H, dk)
        k = k.reshape(B, T, H, dk) * (dk ** -0.5)
        if rotate:
            q, k = _apply_rope(q, cos, sin), _apply_rope(k, cos, sin)
        dirs = tuple((k, jnp.broadcast_to(log_gamma[r][None, None, :, None], (B, T, H, 1)))
                     for r in range(2))
        return q, v.reshape(B, T, H, dv), dirs, g

    ql, vl, dl, gl = project(h_lat, True)
    qc, vc, dc, gc = project(h_ctx, False)
    o_ctx, o_lat = _bidir_scan(qc, vc, dc, ql, vl, dl)

    def readout(o, g):
        B, T = o.shape[:2]
        return (jax.nn.silu(g) * _rms(o).reshape(B, T, 2 * D)) @ w_out

    return readout(o_lat, gl), (readout(o_ctx, gc) if need_ctx else None)


def _hgrn2(h_lat, h_ctx, w_in, w_out, lb_logits, norm_g, layer_idx, need_ctx):
    H, dk, dv = HGRN_HEADS, HGRN_KEY_DIM, HGRN_VAL_DIM
    p = jax.nn.softmax(lb_logits.astype(jnp.float32), axis=0)
    lb = (jnp.cumsum(p, axis=0) - p[0])[layer_idx].reshape(H, dk)

    def project(h):
        B, T, _ = h.shape
        q, i, g, zf, zb = jnp.split(h @ w_in, 5, axis=-1)
        q = jax.nn.silu(q).reshape(B, T, H, dk)

        def gate(z):
            f = lb + (1.0 - lb) * jax.nn.sigmoid(z.astype(jnp.float32).reshape(B, T, H, dk))
            return (1.0 - f, jnp.log(f))

        return q, i.reshape(B, T, H, dv), (gate(zf), gate(zb)), g

    ql, il, dl, gl = project(h_lat)
    qc, ic, dc, gc = project(h_ctx)
    o_ctx, o_lat = _bidir_scan(qc, ic, dc, ql, il, dl)

    def readout(o, g):
        B, T = o.shape[:2]
        return ((_rms(o) * norm_g).reshape(B, T, D_MODEL) * jax.nn.silu(g)) @ w_out

    return readout(o_lat, gl), (readout(o_ctx, gc) if need_ctx else None)


def _gqa(h_lat, h_ctx, w_in, w_out, q_g, k_g, need_ctx):
    Hk, G, d = GQA_KV_HEADS, GQA_GROUP, GQA_HEAD_DIM
    cos, sin = _axial_rope_tables(h_lat.shape[1], d)

    def project(h, rotate):
        B, T, _ = h.shape
        q, k, v = jnp.split(h @ w_in, [D_MODEL, D_MODEL + Hk * d], axis=-1)
        q = _rms(q.reshape(B, T, Hk, G, d)) * q_g * (d ** -0.5)
        k = _rms(k.reshape(B, T, Hk, d)) * k_g
        if rotate:
            q, k = _apply_rope(q, cos, sin), _apply_rope(k, cos, sin)
        return q, k, v.reshape(B, T, Hk, d)

    def attend(q, k, v):
        s = jnp.einsum('bqhgd,bshd->bhgqs', q, k).astype(jnp.float32)
        pr = jax.nn.softmax(s, axis=-1).astype(v.dtype)
        return jnp.einsum('bhgqs,bshd->bqhgd', pr, v)

    def out_proj(o):
        B, T = o.shape[:2]
        return o.reshape(B, T, D_MODEL) @ w_out

    ql, kl, vl = project(h_lat, True)
    qc, kc, vc = project(h_ctx, False)
    k_all = jnp.concatenate([kc, kl], axis=1)
    v_all = jnp.concatenate([vc, vl], axis=1)
    o_lat = _from_blocks(lax.map(lambda qb: attend(qb, k_all, v_all), _to_blocks(ql)))
    o_ctx = out_proj(attend(qc, kc, vc)) if need_ctx else None
    return out_proj(o_lat), o_ctx


def setup_inputs(seed: int = 0) -> dict:
    key = jax.random.key(seed)
    ks = jax.random.split(key, 32)
    D, F = D_MODEL, D_FF
    cnt = [len(range(m, DEPTH, N_MIXERS)) for m in range(N_MIXERS)]
    nA, nB, nC, nD = cnt

    def nrm(i, shape, scale):
        return jax.random.normal(ks[i], shape, jnp.float32) * scale

    gqa_in = D + 2 * GQA_KV_HEADS * GQA_HEAD_DIM
    return {
        "x": nrm(0, (BATCH, SEQ, D), 1.0),
        "c": nrm(1, (BATCH, D), 1.0),
        "ctx": nrm(2, (BATCH, CTX_LEN, D), 1.0),
        "c_ctx": nrm(3, (D,), 1.0),
        "mod_w": nrm(4, (DEPTH, D, N_MOD * D), 0.5 * D ** -0.5),
        "mod_b": nrm(5, (DEPTH, N_MOD * D), 0.02),
        "ffn1_w13": nrm(6, (DEPTH, D, 2 * F), D ** -0.5),
        "ffn1_w2": nrm(7, (DEPTH, F, D), F ** -0.5),
        "ffn2_w13": nrm(8, (DEPTH, D, 2 * F), D ** -0.5),
        "ffn2_w2": nrm(9, (DEPTH, F, D), F ** -0.5),
        "diff_w_in": nrm(10, (nA, D, 3 * D), D ** -0.5),
        "diff_w_out": nrm(11, (nA, D, D), D ** -0.5),
        "diff_lambda": nrm(12, (nA, 4, DIFF_HEAD_DIM), 0.1),
        "diff_subln_g": 1.0 + nrm(13, (nA, 2 * DIFF_HEAD_DIM), 0.02),
        "ret_w_in": nrm(14, (nB, D, 6 * D), D ** -0.5),
        "ret_w_out": nrm(15, (nB, 2 * D, D), (2 * D) ** -0.5),
        "ret_decay_exp": 5.0 + jnp.arange(RET_HEADS, dtype=jnp.float32) + nrm(16, (nB, 2, RET_HEADS), 0.1),
        "hgrn_w_in": nrm(17, (nC, D, 5 * D), D ** -0.5),
        "hgrn_w_out": nrm(18, (nC, D, D), D ** -0.5),
        "hgrn_lb_logits": nrm(19, (DEPTH, D), 0.1),
        "hgrn_norm_g": 1.0 + nrm(20, (nC, HGRN_VAL_DIM), 0.02),
        "gqa_w_in": nrm(21, (nD, D, gqa_in), D ** -0.5),
        "gqa_w_out": nrm(22, (nD, D, D), D ** -0.5),
        "gqa_q_norm_g": 1.0 + nrm(23, (nD, GQA_HEAD_DIM), 0.02),
        "gqa_k_norm_g": 1.0 + nrm(24, (nD, GQA_HEAD_DIM), 0.02),
        "final_norm_g": 1.0 + nrm(25, (D,), 0.02),
    }


def reference(x, c, ctx, c_ctx, mod_w, mod_b, ffn1_w13, ffn1_w2, ffn2_w13, ffn2_w2,
              diff_w_in, diff_w_out, diff_lambda, diff_subln_g,
              ret_w_in, ret_w_out, ret_decay_exp,
              hgrn_w_in, hgrn_w_out, hgrn_lb_logits, hgrn_norm_g,
              gqa_w_in, gqa_w_out, gqa_q_norm_g, gqa_k_norm_g,
              final_norm_g):
    cond_lat = jax.nn.silu(c)
    cond_ctx = jax.nn.silu(c_ctx)
    h, hc = x, ctx
    for i in range(DEPTH):
        kind, j = i % N_MIXERS, i // N_MIXERS
        need_ctx = i < DEPTH - 1
        ml = _split_mod(cond_lat @ mod_w[i] + mod_b[i])
        mc = _split_mod(cond_ctx @ mod_w[i] + mod_b[i])

        h = h + 0.5 * ml[2] * _swiglu(_modulate(h, ml[0], ml[1]), ffn1_w13[i], ffn1_w2[i])
        hc = hc + 0.5 * mc[2] * _swiglu(_modulate(hc, mc[0], mc[1]), ffn1_w13[i], ffn1_w2[i])

        a_l = _modulate(h, ml[3], ml[4])
        a_c = _modulate(hc, mc[3], mc[4])
        if kind == 0:
            o_l, o_c = _diff_attention(a_l, a_c, diff_w_in[j], diff_w_out[j], diff_lambda[j],
                                       diff_subln_g[j], i, need_ctx)
        elif kind == 1:
            o_l, o_c = _retention(a_l, a_c, ret_w_in[j], ret_w_out[j], ret_decay_exp[j], need_ctx)
        elif kind == 2:
            o_l, o_c = _hgrn2(a_l, a_c, hgrn_w_in[j], hgrn_w_out[j], hgrn_lb_logits,
                              hgrn_norm_g[j], i, need_ctx)
        else:
            o_l, o_c = _gqa(a_l, a_c, gqa_w_in[j], gqa_w_out[j], gqa_q_norm_g[j],
                            gqa_k_norm_g[j], need_ctx)
        h = h + ml[5] * o_l

        h = h + 0.5 * ml[8] * _swiglu(_modulate(h, ml[6], ml[7]), ffn2_w13[i], ffn2_w2[i])
        if need_ctx:
            hc = hc + mc[5] * o_c
            hc = hc + 0.5 * mc[8] * _swiglu(_modulate(hc, mc[6], mc[7]), ffn2_w13[i], ffn2_w2[i])
    return _rms(h) * final_norm_g
```

```python
import functools
import math

import jax
import jax.numpy as jnp
from jax import lax
from jax.experimental import pallas as pl
from jax.experimental.pallas import tpu as pltpu

F32 = jnp.float32
BF16 = jnp.bfloat16

NORM_EPS = 1e-6
ROPE_THETA = 10000.0
GRID_W = 64
N_MOD = 9
N_MIXERS = 4

LANES = 128
MOD_ROWS = 8
VMEM_LIMIT = 56 * 1024 * 1024

DIFF_HEADS = 8
RET_HEADS = 4
GQA_KV_HEADS = 2
GQA_HEAD_DIM = 128
HGRN_HEAD_DIM = 128
FFN_CHUNK = 256
RET_CHUNK = 256
HGRN_CHUNK = 64
HGRN_BLOCK = 256
ATTN_KV_CHUNK = 512


def _params(*sem):
    return pltpu.CompilerParams(dimension_semantics=sem, vmem_limit_bytes=VMEM_LIMIT)


def _dot(a, b):
    return jnp.dot(a, b, preferred_element_type=F32)


def _dot_nt(a, b):
    return lax.dot_general(a, b, (((1,), (1,)), ((), ())), preferred_element_type=F32)


def _dot_tn(a, b):
    return lax.dot_general(a, b, (((0,), (0,)), ((), ())), preferred_element_type=F32)


def _silu(x):
    return x * jax.nn.sigmoid(x)


def _rms(x):
    return x * lax.rsqrt(jnp.mean(x * x, axis=-1, keepdims=True) + NORM_EPS)


def _modulated(h, mod_ref, k0):
    shift = mod_ref[k0:k0 + 1, :]
    scale = mod_ref[k0 + 1:k0 + 2, :]
    return _rms(h) * (1.0 + scale) + shift


def _rope(y, c, s1, s2, nq):
    if 2 * nq == LANES:
        return y * c + pltpu.roll(y, nq, 1) * s1
    return y * c + pltpu.roll(y, LANES - nq, 1) * s1 + pltpu.roll(y, nq, 1) * s2


class _Plan:
    def __init__(self, B, T, L):
        self.B, self.T, self.L = B, T, L
        self.n_lat = B * T
        self.n_ctx = B * L
        self.NT = self.n_lat + self.n_ctx
        for tm in (1024, 512, 256):
            if T % tm == 0 and self.n_ctx % tm == 0:
                self.TM = tm
                break
        else:
            raise ValueError("unsupported sequence lengths")
        assert L == RET_CHUNK == HGRN_BLOCK and T % L == 0 and T % GRID_W == 0
        assert B + 1 <= MOD_ROWS
        self.tiles_per_batch = T // self.TM
        self.lat_tiles = self.n_lat // self.TM
        self.all_tiles = self.NT // self.TM

    def mod_row(self, i):
        return jnp.where(i < self.lat_tiles, i // self.tiles_per_batch, self.B)

    def pos_block(self, i):
        return jnp.where(i < self.lat_tiles, i % self.tiles_per_batch, self.tiles_per_batch)


def _rope_tables(plan, d, width):
    T = plan.T
    rows = T // GRID_W
    row = jnp.repeat(jnp.arange(rows, dtype=F32), GRID_W)
    col = jnp.tile(jnp.arange(GRID_W, dtype=F32), rows)
    nq = d // 4
    inv = ROPE_THETA ** (-jnp.arange(nq, dtype=F32) * 2.0 / (d // 2))
    ar, ac = row[:, None] * inv, col[:, None] * inv
    cr, sr, cc, sc = jnp.cos(ar), jnp.sin(ar), jnp.cos(ac), jnp.sin(ac)
    z = jnp.zeros_like(sr)
    c = jnp.concatenate([cr, cr, cc, cc], axis=1)
    s1 = jnp.concatenate([-sr, z, -sc, z], axis=1)
    s2 = jnp.concatenate([z, sr, z, sc], axis=1)
    if 2 * nq == LANES:
        s1, s2 = s1 + s2, None
    reps = width // d

    def finish(t, fill):
        t = jnp.tile(t, (1, reps))
        return jnp.concatenate([t, jnp.full((plan.TM, width), fill, F32)], axis=0)

    return finish(c, 1.0), finish(s1, 0.0), (None if s2 is None else finish(s2, 0.0))


def _mod_kernel(c_ref, w_ref, b_ref, o_ref):
    cond = _silu(c_ref[...]).astype(BF16)
    o_ref[...] = _dot(cond, w_ref[...].astype(BF16)) + b_ref[...]


def _mod_call(cstack, mod_w, mod_b):
    depth, D, _ = mod_w.shape
    out = pl.pallas_call(
        _mod_kernel,
        out_shape=jax.ShapeDtypeStruct((depth, MOD_ROWS, N_MOD * D), F32),
        grid=(depth, N_MOD),
        in_specs=[
            pl.BlockSpec((MOD_ROWS, D), lambda l, n: (0, 0)),
            pl.BlockSpec((None, D, D), lambda l, n: (l, 0, n)),
            pl.BlockSpec((None, 1, D), lambda l, n: (l, 0, n)),
        ],
        out_specs=pl.BlockSpec((None, MOD_ROWS, D), lambda l, n: (l, 0, n)),
        compiler_params=_params("arbitrary", "arbitrary"),
        name="mod_table",
    )(cstack, mod_w, mod_b.reshape(depth, 1, N_MOD * D))
    return out.reshape(depth, MOD_ROWS, N_MOD, D)


def _ffn_kernel(*refs, k0, nf, final):
    if final:
        h_ref, mod_ref, w1_ref, w3_ref, w2_ref, g_ref, o_ref, xn_ref, acc_ref = refs
    else:
        h_ref, mod_ref, w1_ref, w3_ref, w2_ref, o_ref, xn_ref, acc_ref = refs
    j = pl.program_id(1)

    @pl.when(j == 0)
    def _():
        xn_ref[...] = _modulated(h_ref[...], mod_ref, k0).astype(BF16)
        acc_ref[...] = jnp.zeros_like(acc_ref)

    xn = xn_ref[...]
    a = _dot(xn, w1_ref[...])
    b = _dot(xn, w3_ref[...])
    acc_ref[...] += _dot((_silu(a) * b).astype(BF16), w2_ref[...])

    @pl.when(j == nf - 1)
    def _():
        gate = mod_ref[k0 + 2:k0 + 3, :]
        hn = h_ref[...] + 0.5 * gate * acc_ref[...]
        if final:
            hn = _rms(hn) * g_ref[...]
        o_ref[...] = hn


def _ffn_call(H, mod, layer, k0, w13, w2, n_tiles, plan, final_g=None):
    NT, D = H.shape
    F = w2.shape[0]
    nf = F // FFN_CHUNK
    TM = plan.TM
    final = final_g is not None
    in_specs = [
        pl.BlockSpec((TM, D), lambda i, j: (i, 0)),
        pl.BlockSpec((None, None, N_MOD, D), lambda i, j: (layer, plan.mod_row(i), 0, 0)),
        pl.BlockSpec((D, FFN_CHUNK), lambda i, j: (0, j)),
        pl.BlockSpec((D, FFN_CHUNK), lambda i, j: (0, j + nf)),
        pl.BlockSpec((FFN_CHUNK, D), lambda i, j: (j, 0)),
    ]
    args = [H, mod, w13, w13, w2]
    if final:
        in_specs.append(pl.BlockSpec((1, D), lambda i, j: (0, 0)))
        args.append(final_g.reshape(1, D))
        out_shape = jax.ShapeDtypeStruct((n_tiles * TM, D), F32)
        aliases = {}
    else:
        out_shape = jax.ShapeDtypeStruct((NT, D), F32)
        aliases = {0: 0}
    return pl.pallas_call(
        functools.partial(_ffn_kernel, k0=k0, nf=nf, final=final),
        out_shape=out_shape,
        grid=(n_tiles, nf),
        in_specs=in_specs,
        out_specs=pl.BlockSpec((TM, D), lambda i, j: (i, 0)),
        scratch_shapes=[pltpu.VMEM((TM, D), BF16), pltpu.VMEM((TM, D), F32)],
        input_output_aliases=aliases,
        compiler_params=_params("arbitrary", "arbitrary"),
        name="ffn_final" if final else "ffn",
    )(*args)


PROJ_COLS = 256


def _proj_diff_kernel(h_ref, mod_ref, w_ref, c_ref, s1_ref, s2_ref, q_ref, k_ref, v_ref, *, D, d):
    xm = _modulated(h_ref[...], mod_ref, 3).astype(BF16)
    c, s1, s2 = c_ref[...], s1_ref[...], s2_ref[...]
    nq = d // 4
    for n in range(0, D, PROJ_COLS):
        yq = _dot(xm, w_ref[:, n:n + PROJ_COLS]) * (d ** -0.5)
        yk = _dot(xm, w_ref[:, D + n:D + n + PROJ_COLS])
        for m in range(0, PROJ_COLS, LANES):
            q_ref[:, n + m:n + m + LANES] = _rope(yq[:, m:m + LANES], c, s1, s2, nq).astype(BF16)
            k_ref[:, n + m:n + m + LANES] = _rope(yk[:, m:m + LANES], c, s1, s2, nq).astype(BF16)
        v_ref[:, n:n + PROJ_COLS] = _dot(xm, w_ref[:, 2 * D + n:2 * D + n + PROJ_COLS]).astype(BF16)


def _proj_ret_kernel(h_ref, mod_ref, w_ref, c_ref, s_ref, q_ref, k_ref, v_ref, g_ref, *, D, dk):
    xm = _modulated(h_ref[...], mod_ref, 3).astype(BF16)
    c, s = c_ref[...], s_ref[...]
    nq = dk // 4
    for n in range(0, D, PROJ_COLS):
        yq = _dot(xm, w_ref[:, n:n + PROJ_COLS])
        yk = _dot(xm, w_ref[:, D + n:D + n + PROJ_COLS]) * (dk ** -0.5)
        for m in range(0, PROJ_COLS, LANES):
            cm, sm = c[:, m:m + LANES], s[:, m:m + LANES]
            q_ref[:, n + m:n + m + LANES] = _rope(yq[:, m:m + LANES], cm, sm, None, nq).astype(BF16)
            k_ref[:, n + m:n + m + LANES] = _rope(yk[:, m:m + LANES], cm, sm, None, nq).astype(BF16)
    for n in range(0, 2 * D, PROJ_COLS):
        v_ref[:, n:n + PROJ_COLS] = _dot(xm, w_ref[:, 2 * D + n:2 * D + n + PROJ_COLS]).astype(BF16)
        g_ref[:, n:n + PROJ_COLS] = _silu(_dot(xm, w_ref[:, 4 * D + n:4 * D + n + PROJ_COLS])).astype(BF16)


def _proj_hgrn_kernel(h_ref, mod_ref, w_ref, q_ref, i_ref, g_ref, zf_ref, zb_ref, *, D):
    xm = _modulated(h_ref[...], mod_ref, 3).astype(BF16)
    for n in range(0, D, PROJ_COLS):
        sl = slice(n, n + PROJ_COLS)
        q_ref[:, sl] = _silu(_dot(xm, w_ref[:, n:n + PROJ_COLS])).astype(BF16)
        i_ref[:, sl] = _dot(xm, w_ref[:, D + n:D + n + PROJ_COLS]).astype(BF16)
        g_ref[:, sl] = _silu(_dot(xm, w_ref[:, 2 * D + n:2 * D + n + PROJ_COLS])).astype(BF16)
        zf_ref[:, sl] = _dot(xm, w_ref[:, 3 * D + n:3 * D + n + PROJ_COLS])
        zb_ref[:, sl] = _dot(xm, w_ref[:, 4 * D + n:4 * D + n + PROJ_COLS])


def _proj_gqa_kernel(h_ref, mod_ref, w_ref, c_ref, s1_ref, s2_ref, qg_ref, kg_ref,
                     q_ref, k_ref, v_ref, *, D, d, kvw):
    xm = _modulated(h_ref[...], mod_ref, 3).astype(BF16)
    c, s1, s2 = c_ref[...], s1_ref[...], s2_ref[...]
    nq = d // 4
    qg = qg_ref[...] * (d ** -0.5)
    kg = kg_ref[...]
    for n in range(0, D, PROJ_COLS):
        yq = _dot(xm, w_ref[:, n:n + PROJ_COLS])
        for m in range(0, PROJ_COLS, LANES):
            q_ref[:, n + m:n + m + LANES] = _rope(_rms(yq[:, m:m + LANES]) * qg, c, s1, s2, nq).astype(BF16)
    yk = _dot(xm, w_ref[:, D:D + kvw])
    for m in range(0, kvw, LANES):
        k_ref[:, m:m + LANES] = _rope(_rms(yk[:, m:m + LANES]) * kg, c, s1, s2, nq).astype(BF16)
    v_ref[...] = _dot(xm, w_ref[:, D + kvw:D + 2 * kvw]).astype(BF16)


def _proj_call(kernel, H, mod, layer, w, extra, extra_specs, outs, plan, name):
    NT, D = H.shape
    TM = plan.TM
    in_specs = [
        pl.BlockSpec((TM, D), lambda i: (i, 0)),
        pl.BlockSpec((None, None, N_MOD, D), lambda i: (layer, plan.mod_row(i), 0, 0)),
        pl.BlockSpec(w.shape, lambda i: (0, 0)),
    ] + extra_specs
    return pl.pallas_call(
        kernel,
        out_shape=[jax.ShapeDtypeStruct((NT, wd), dt) for wd, dt in outs],
        grid=(plan.all_tiles,),
        in_specs=in_specs,
        out_specs=[pl.BlockSpec((TM, wd), lambda i: (i, 0)) for wd, _ in outs],
        compiler_params=_params("arbitrary"),
        name=name,
    )(H, mod, w, *extra)


def _table_specs(plan, tables):
    return [pl.BlockSpec((plan.TM, t.shape[1]), lambda i: (plan.pos_block(i), 0)) for t in tables]


def _flash(q, sources):
    m = l = acc = None
    for k_ref, v_ref, start, size in sources:
        k = k_ref[start:start + size, :]
        v = v_ref[start:start + size, :]
        s = _dot_nt(q, k)
        ms = jnp.max(s, axis=1, keepdims=True)
        if m is None:
            m = ms
            p = jnp.exp(s - m)
            l = jnp.sum(p, axis=1, keepdims=True)
            acc = _dot(p.astype(BF16), v)
        else:
            m_new = jnp.maximum(m, ms)
            alpha = jnp.exp(m - m_new)
            p = jnp.exp(s - m_new)
            l = alpha * l + jnp.sum(p, axis=1, keepdims=True)
            acc = alpha * acc + _dot(p.astype(BF16), v)
            m = m_new
    return acc, l


def _kv_sources(kv_refs, L, T):
    if len(kv_refs) == 2:
        kc, vc = kv_refs
        return [(kc, vc, 0, L)]
    kc, vc, kl, vl = kv_refs
    chunk = min(ATTN_KV_CHUNK, T)
    return [(kc, vc, 0, L)] + [(kl, vl, s, chunk) for s in range(0, T, chunk)]


def _diff_attn_kernel(*refs, lam_init, L, T):
    lam_ref, g_ref, q_ref = refs[:3]
    kv_refs, o_ref = refs[3:-1], refs[-1]
    q = q_ref[...]
    tq = q.shape[0]
    lane = lax.broadcasted_iota(jnp.int32, q.shape, 1)
    zero = jnp.zeros_like(q)
    half = LANES // 2
    qs = jnp.concatenate([jnp.where(lane < half, q, zero), jnp.where(lane >= half, q, zero)], axis=0)
    acc, l = _flash(qs, _kv_sources(kv_refs, L, T))
    lam = lam_ref[...]
    lam_full = (jnp.exp(jnp.sum(lam[0:1] * lam[1:2], axis=1, keepdims=True))
                - jnp.exp(jnp.sum(lam[2:3] * lam[3:4], axis=1, keepdims=True)) + lam_init)
    o = acc[:tq] / l[:tq] - lam_full * (acc[tq:] / l[tq:])
    o_ref[...] = (_rms(o) * g_ref[...] * (1.0 - lam_init)).astype(BF16)


def _gqa_attn_kernel(*refs, G, L, T):
    q_ref = refs[0]
    kv_refs, o_ref = refs[1:-1], refs[-1]
    tq = q_ref.shape[0]
    qs = jnp.concatenate([q_ref[:, g * LANES:(g + 1) * LANES] for g in range(G)], axis=0)
    acc, l = _flash(qs, _kv_sources(kv_refs, L, T))
    o = acc / l
    for g in range(G):
        o_ref[:, g * LANES:(g + 1) * LANES] = o[g * tq:(g + 1) * tq].astype(BF16)


def _attn_call(kernel, plan, q, k, v, kv_col, n_heads, q_width, small, *, ctx_out=None, name):
    B, T, L = plan.B, plan.T, plan.L
    NT = q.shape[0]
    ctx_row = plan.n_lat // L
    kvw = LANES
    small_specs = [pl.BlockSpec(s.shape, lambda b, h, i: (0, 0)) for s in small]
    kv_specs = [pl.BlockSpec((L, kvw), lambda b, h, i: (ctx_row + b, kv_col(h)))] * 2
    args = list(small) + [q, k, v]
    if ctx_out is None:
        tq = min(256, T)
        nq = T // tq
        q_spec = pl.BlockSpec((tq, q_width), lambda b, h, i: (b * nq + i, h))
        kv_specs = kv_specs + [pl.BlockSpec((T, kvw), lambda b, h, i: (b, kv_col(h)))] * 2
        args += [k, v]
        aliases = {}
    else:
        tq, nq = L, 1
        q_spec = pl.BlockSpec((tq, q_width), lambda b, h, i: (ctx_row + b, h))
        args.append(ctx_out)
        kv_specs = kv_specs + [pl.BlockSpec(memory_space=pl.ANY)]
        aliases = {len(args) - 1: 0}
    if ctx_out is not None:
        body = lambda *r: kernel(*r[:-2], r[-1])
    else:
        body = kernel
    return pl.pallas_call(
        body,
        out_shape=jax.ShapeDtypeStruct((NT, q.shape[1]), BF16),
        grid=(B, n_heads, nq),
        in_specs=small_specs + [q_spec] + kv_specs,
        out_specs=q_spec,
        input_output_aliases=aliases,
        compiler_params=_params("arbitrary", "arbitrary", "arbitrary"),
        name=name,
    )(*args)


def _ret_kernel(dec_ref, qc_ref, kc_ref, vc_ref, gc_ref, ql_ref, kl_ref, vl_ref, gl_ref,
                yc_ref, yl_ref, sb_ref, st_ref, *, C, n_chunks):
    lg = jnp.log(1.0 - jnp.exp(-dec_ref[...] * math.log(2.0)))
    lgf, lgb = lg[0:1], lg[1:2]
    i = lax.broadcasted_iota(jnp.int32, (C, 1), 0).astype(F32)
    qdf, kef = jnp.exp((i + 1.0) * lgf), jnp.exp((C - 1.0 - i) * lgf)
    qdb, keb = jnp.exp((C - i) * lgb), jnp.exp(i * lgb)
    gfc, gbc = jnp.exp(C * lgf), jnp.exp(C * lgb)
    dist = (lax.broadcasted_iota(jnp.int32, (C, C), 0) - lax.broadcasted_iota(jnp.int32, (C, C), 1)).astype(F32)
    w = jnp.where(dist > 0, jnp.exp(jnp.maximum(dist, 0.0) * lgf),
                  jnp.where(dist < 0, jnp.exp(jnp.maximum(-dist, 0.0) * lgb), 2.0))

    def intra(q, k, v):
        return _dot((_dot_nt(q, k) * w).astype(BF16), v)

    def kv_state(k, v, ke):
        return _dot_tn((k.astype(F32) * ke).astype(BF16), v)

    def readout(o, g):
        return (g.astype(F32) * _rms(o)).astype(BF16)

    qx, kx, vx = qc_ref[...], kc_ref[...], vc_ref[...]
    yc_ref[...] = readout(intra(qx, kx, vx), gc_ref[...])

    st_ref[...] = kv_state(kx, vx, keb)

    def bwd(t, carry):
        c = n_chunks - 1 - t
        rows = pl.ds(pl.multiple_of(c * C, C), C)
        s = st_ref[...]
        sb_ref[c] = s.astype(BF16)
        st_ref[...] = gbc * s + kv_state(kl_ref[rows, :], vl_ref[rows, :], keb)
        return carry

    lax.fori_loop(0, n_chunks, bwd, 0)

    st_ref[...] = kv_state(kx, vx, kef)

    def fwd(c, carry):
        rows = pl.ds(pl.multiple_of(c * C, C), C)
        q, k, v = ql_ref[rows, :], kl_ref[rows, :], vl_ref[rows, :]
        qf = q.astype(F32)
        s = st_ref[...]
        o = (intra(q, k, v) + _dot((qf * qdf).astype(BF16), s.astype(BF16))
             + _dot((qf * qdb).astype(BF16), sb_ref[c]))
        yl_ref[rows, :] = readout(o, gl_ref[rows, :])
        st_ref[...] = gfc * s + kv_state(k, v, kef)
        return carry

    lax.fori_loop(0, n_chunks, fwd, 0)


def _ret_call(plan, q, k, v, g, decay_exp):
    B, T, L = plan.B, plan.T, plan.L
    NT, D = q.shape
    H = RET_HEADS
    dk, dv = D // H, v.shape[1] // H
    C = RET_CHUNK
    n_chunks = T // C
    ctx_row = plan.n_lat // L
    dec = jnp.transpose(decay_exp.astype(F32)).reshape(H, 2, 1)
    ctx = lambda w: pl.BlockSpec((L, w), lambda b, h: (ctx_row + b, h))
    lat = lambda w: pl.BlockSpec((T, w), lambda b, h: (b, h))
    yc, yl = pl.pallas_call(
        functools.partial(_ret_kernel, C=C, n_chunks=n_chunks),
        out_shape=[jax.ShapeDtypeStruct((plan.n_ctx, v.shape[1]), BF16),
                   jax.ShapeDtypeStruct((plan.n_lat, v.shape[1]), BF16)],
        grid=(B, H),
        in_specs=[pl.BlockSpec((None, 2, 1), lambda b, h: (h, 0, 0)),
                  ctx(dk), ctx(dk), ctx(dv), ctx(dv), lat(dk), lat(dk), lat(dv), lat(dv)],
        out_specs=[pl.BlockSpec((L, dv), lambda b, h: (b, h)),
                   pl.BlockSpec((T, dv), lambda b, h: (b, h))],
        scratch_shapes=[pltpu.VMEM((n_chunks, dk, dv), BF16), pltpu.VMEM((dk, dv), F32)],
        compiler_params=_params("arbitrary", "arbitrary"),
        name="retention",
    )(dec, q, k, v, g, q, k, v, g)
    return jnp.concatenate([yl, yc], axis=0)


def _hgrn_lower_bound(logits, layer):
    e = jnp.exp(logits - jnp.max(logits, axis=0, keepdims=True))
    p = e / jnp.sum(e, axis=0, keepdims=True)
    return jnp.sum(p[0:layer + 1], axis=0, keepdims=True) - p[0:1]


def _hgrn_kernel(lb_ref, qf_ref, vf_ref, zf_ref, qb_ref, vb_ref, zb_ref, of_ref, ob_ref,
                 sf_ref, sb_ref, *, layer, C, n_heads):
    @pl.when(pl.program_id(1) == 0)
    def _():
        sf_ref[...] = jnp.zeros_like(sf_ref)
        sb_ref[...] = jnp.zeros_like(sb_ref)

    lb = _hgrn_lower_bound(lb_ref[...], layer)
    r_i = lax.broadcasted_iota(jnp.int32, (C, C), 0)
    c_i = lax.broadcasted_iota(jnp.int32, (C, C), 1)
    lower = r_i >= c_i
    upper = r_i <= c_i
    n_sub = qf_ref.shape[0] // C

    def chunk(q_ref, v_ref, z_ref, o_ref, st_ref, rows, mask, mid, last):
        q, v, z = q_ref[rows, :].astype(F32), v_ref[rows, :], z_ref[rows, :]
        f = lb + (1.0 - lb) * jax.nn.sigmoid(z)
        kk = 1.0 - f
        b = jnp.dot(mask.astype(F32), jnp.log(f), preferred_element_type=F32,
                    precision=lax.Precision.HIGHEST)
        ref, tot = b[mid:mid + 1], b[last:last + 1]
        qd, kd = (q * jnp.exp(b - ref)).astype(BF16), (kk * jnp.exp(ref - b)).astype(BF16)
        qs, ke = (q * jnp.exp(b)).astype(BF16), (kk * jnp.exp(tot - b)).astype(BF16)
        dec = jnp.exp(tot)
        for h in range(n_heads):
            sl = slice(h * HGRN_HEAD_DIM, (h + 1) * HGRN_HEAD_DIM)
            att = jnp.where(mask, _dot_nt(qd[:, sl], kd[:, sl]), 0.0).astype(BF16)
            st = st_ref[h]
            o_ref[rows, sl] = _dot(att, v[:, sl]) + _dot_nt(qs[:, sl], st.astype(BF16))
            st_ref[h] = st * dec[:, sl] + _dot_tn(v[:, sl], ke[:, sl])

    for j in range(n_sub):
        chunk(qf_ref, vf_ref, zf_ref, of_ref, sf_ref, slice(j * C, (j + 1) * C), lower, C // 2 - 1, C - 1)
        jb = n_sub - 1 - j
        chunk(qb_ref, vb_ref, zb_ref, ob_ref, sb_ref, slice(jb * C, (jb + 1) * C), upper, C // 2, 0)


def _hgrn_call(plan, layer, lb_logits, q, v, zf, zb):
    B, T, L = plan.B, plan.T, plan.L
    NT, D = q.shape
    TB = HGRN_BLOCK
    nb = T // TB
    ctx_blk = plan.n_lat // TB
    H = D // HGRN_HEAD_DIM
    fwd = lambda b, s: (jnp.where(s == 0, ctx_blk + b, b * nb + s - 1), 0)
    bwd = lambda b, s: (jnp.where(s == 0, ctx_blk + b, b * nb + nb - s), 0)
    blk = lambda im: pl.BlockSpec((TB, D), im)
    return pl.pallas_call(
        functools.partial(_hgrn_kernel, layer=layer, C=HGRN_CHUNK, n_heads=H),
        out_shape=[jax.ShapeDtypeStruct((NT, D), F32), jax.ShapeDtypeStruct((NT, D), F32)],
        grid=(B, nb + 1),
        in_specs=[pl.BlockSpec(lb_logits.shape, lambda b, s: (0, 0)),
                  blk(fwd), blk(fwd), blk(fwd), blk(bwd), blk(bwd), blk(bwd)],
        out_specs=[blk(fwd), blk(bwd)],
        scratch_shapes=[pltpu.VMEM((H, HGRN_HEAD_DIM, HGRN_HEAD_DIM), F32),
                        pltpu.VMEM((H, HGRN_HEAD_DIM, HGRN_HEAD_DIM), F32)],
        compiler_params=_params("arbitrary", "arbitrary"),
        name="hgrn2",
    )(lb_logits, q, v, zf, q, v, zb)


def _oproj_kernel(h_ref, mod_ref, y_ref, w_ref, o_ref):
    o_ref[...] = h_ref[...] + mod_ref[5:6, :] * _dot(y_ref[...], w_ref[...])


def _oproj_hgrn_kernel(h_ref, mod_ref, of_ref, ob_ref, g_ref, ng_ref, w_ref, o_ref):
    o = of_ref[...] + ob_ref[...]
    ng = ng_ref[...]
    ys = []
    for n in range(0, o.shape[1], HGRN_HEAD_DIM):
        sl = slice(n, n + HGRN_HEAD_DIM)
        ys.append((_rms(o[:, sl]) * ng * g_ref[:, sl].astype(F32)).astype(BF16))
    y = jnp.concatenate(ys, axis=1)
    o_ref[...] = h_ref[...] + mod_ref[5:6, :] * _dot(y, w_ref[...])


def _oproj_call(kernel, H, mod, layer, ys, small, w, n_tiles, plan, name):
    NT, D = H.shape
    TM = plan.TM
    in_specs = ([pl.BlockSpec((TM, D), lambda i: (i, 0)),
                 pl.BlockSpec((None, None, N_MOD, D), lambda i: (layer, plan.mod_row(i), 0, 0))]
                + [pl.BlockSpec((TM, y.shape[1]), lambda i: (i, 0)) for y in ys]
                + [pl.BlockSpec(s.shape, lambda i: (0, 0)) for s in small]
                + [pl.BlockSpec(w.shape, lambda i: (0, 0))])
    return pl.pallas_call(
        kernel,
        out_shape=jax.ShapeDtypeStruct((NT, D), F32),
        grid=(n_tiles,),
        in_specs=in_specs,
        out_specs=pl.BlockSpec((TM, D), lambda i: (i, 0)),
        input_output_aliases={0: 0},
        compiler_params=_params("arbitrary"),
        name=name,
    )(H, mod, *ys, *small, w)


def kernel(x, c, ctx, c_ctx, mod_w, mod_b, ffn1_w13, ffn1_w2, ffn2_w13, ffn2_w2, diff_w_in, diff_w_out, diff_lambda, diff_subln_g, ret_w_in, ret_w_out, ret_decay_exp, hgrn_w_in, hgrn_w_out, hgrn_lb_logits, hgrn_norm_g, gqa_w_in, gqa_w_out, gqa_q_norm_g, gqa_k_norm_g, final_norm_g):
    B, T, D = x.shape
    L = ctx.shape[1]
    depth = mod_w.shape[0]
    plan = _Plan(B, T, L)

    cstack = jnp.concatenate([c, c_ctx[None, :], jnp.zeros((MOD_ROWS - B - 1, D), F32)], axis=0)
    mod = _mod_call(cstack, mod_w, mod_b)
    H = jnp.concatenate([x.reshape(B * T, D), ctx.reshape(B * L, D)], axis=0)
    bf = lambda w: w.astype(BF16)

    for i in range(depth):
        kind, j = i % N_MIXERS, i // N_MIXERS
        need_ctx = i < depth - 1
        mix_tiles = plan.all_tiles if need_ctx else plan.lat_tiles

        H = _ffn_call(H, mod, i, 0, bf(ffn1_w13[i]), bf(ffn1_w2[i]), plan.all_tiles, plan)

        if kind == 0:
            d = D // DIFF_HEADS // 2
            lam_init = 0.8 - 0.6 * math.exp(-0.3 * i)
            tabs = _rope_tables(plan, d, LANES)
            q, k, v = _proj_call(
                functools.partial(_proj_diff_kernel, D=D, d=d), H, mod, i, bf(diff_w_in[j]),
                tabs, _table_specs(plan, tabs), [(D, BF16)] * 3, plan, "proj_diff")
            small = [diff_lambda[j], diff_subln_g[j].reshape(1, 2 * d)]
            kern = functools.partial(_diff_attn_kernel, lam_init=lam_init, L=L, T=T)
            y = _attn_call(kern, plan, q, k, v, lambda h: h, DIFF_HEADS, LANES, small, name="diff_attn")
            if need_ctx:
                y = _attn_call(kern, plan, q, k, v, lambda h: h, DIFF_HEADS, LANES, small,
                               ctx_out=y, name="diff_attn_ctx")
            H = _oproj_call(_oproj_kernel, H, mod, i, [y], [], bf(diff_w_out[j]), mix_tiles, plan, "oproj")
        elif kind == 1:
            dk = D // RET_HEADS
            c_t, s_t, _ = _rope_tables(plan, dk, dk)
            tabs = [c_t, s_t]
            q, k, v, g = _proj_call(
                functools.partial(_proj_ret_kernel, D=D, dk=dk), H, mod, i, bf(ret_w_in[j]),
                tabs, _table_specs(plan, tabs), [(D, BF16), (D, BF16), (2 * D, BF16), (2 * D, BF16)],
                plan, "proj_ret")
            y = _ret_call(plan, q, k, v, g, ret_decay_exp[j])
            H = _oproj_call(_oproj_kernel, H, mod, i, [y], [], bf(ret_w_out[j]), mix_tiles, plan, "oproj")
        elif kind == 2:
            q, v, g, zf, zb = _proj_call(
                functools.partial(_proj_hgrn_kernel, D=D), H, mod, i, bf(hgrn_w_in[j]),
                [], [], [(D, BF16), (D, BF16), (D, BF16), (D, F32), (D, F32)], plan, "proj_hgrn")
            o_f, o_b = _hgrn_call(plan, i, hgrn_lb_logits, q, v, zf, zb)
            H = _oproj_call(_oproj_hgrn_kernel, H, mod, i, [o_f, o_b, g],
                            [hgrn_norm_g[j].reshape(1, HGRN_HEAD_DIM)], bf(hgrn_w_out[j]),
                            mix_tiles, plan, "oproj_hgrn")
        else:
            d = GQA_HEAD_DIM
            kvw = GQA_KV_HEADS * d
            G = D // d // GQA_KV_HEADS
            tabs = _rope_tables(plan, d, LANES)
            extra = list(tabs) + [gqa_q_norm_g[j].reshape(1, d), gqa_k_norm_g[j].reshape(1, d)]
            specs = _table_specs(plan, tabs) + [pl.BlockSpec((1, d), lambda t: (0, 0))] * 2
            q, k, v = _proj_call(
                functools.partial(_proj_gqa_kernel, D=D, d=d, kvw=kvw), H, mod, i, bf(gqa_w_in[j]),
                extra, specs, [(D, BF16), (kvw, BF16), (kvw, BF16)], plan, "proj_gqa")
            kern = functools.partial(_gqa_attn_kernel, G=G, L=L, T=T)
            y = _attn_call(kern, plan, q, k, v, lambda h: h, GQA_KV_HEADS, G * d, [], name="gqa_attn")
            if need_ctx:
                y = _attn_call(kern, plan, q, k, v, lambda h: h, GQA_KV_HEADS, G * d, [],
                               ctx_out=y, name="gqa_attn_ctx")
            H = _oproj_call(_oproj_kernel, H, mod, i, [y], [], bf(gqa_w_out[j]), mix_tiles, plan, "oproj")

        if need_ctx:
            H = _ffn_call(H, mod, i, 6, bf(ffn2_w13[i]), bf(ffn2_w2[i]), plan.all_tiles, plan)
        else:
            H = _ffn_call(H, mod, i, 6, bf(ffn2_w13[i]), bf(ffn2_w2[i]), plan.lat_tiles, plan,
                          final_g=final_norm_g)
    return H.reshape(B, T, D)
```

```python
import functools
import math

import jax
import jax.numpy as jnp
import numpy as np
from jax import lax
from jax.experimental import pallas as pl
from jax.experimental.pallas import tpu as pltpu

F32 = jnp.float32
BF16 = jnp.bfloat16

NORM_EPS = 1e-6
ROPE_THETA = 10000.0
GRID_W = 64
N_MOD = 9
N_MIXERS = 4
LOG2E = math.log2(math.e)

LANES = 128
MOD_ROWS = 8
VMEM_LIMIT = 56 * 1024 * 1024

DIFF_HEADS = 8
RET_HEADS = 4
GQA_KV_HEADS = 2
GQA_HEAD_DIM = 128
HGRN_HEAD_DIM = 128
FFN_CHUNK = 256
RET_CHUNK = 256
HGRN_CHUNK = 64
HGRN_BLOCK = 256
ATTN_KV_CHUNK = 512
DIFF_TQ = 1024
GQA_TQ = 512


def _params(*sem):
    return pltpu.CompilerParams(dimension_semantics=sem, vmem_limit_bytes=VMEM_LIMIT)


def _dot(a, b):
    return jnp.dot(a, b, preferred_element_type=F32)


def _dot_nt(a, b):
    return lax.dot_general(a, b, (((1,), (1,)), ((), ())), preferred_element_type=F32)


def _dot_tn(a, b):
    return lax.dot_general(a, b, (((0,), (0,)), ((), ())), preferred_element_type=F32)


def _silu(x):
    return x * jax.nn.sigmoid(x)


def _rms(x):
    return x * lax.rsqrt(jnp.mean(x * x, axis=-1, keepdims=True) + NORM_EPS)


def _modulated(h, mod_ref, k0):
    shift = mod_ref[k0:k0 + 1, :]
    scale = mod_ref[k0 + 1:k0 + 2, :]
    return _rms(h) * (1.0 + scale) + shift


def _rope(y, c, s1, s2, nq):
    if 2 * nq == LANES:
        return y * c + pltpu.roll(y, nq, 1) * s1
    return y * c + pltpu.roll(y, LANES - nq, 1) * s1 + pltpu.roll(y, nq, 1) * s2


class _Plan:
    def __init__(self, B, T, L):
        self.B, self.T, self.L = B, T, L
        self.n_lat = B * T
        self.n_ctx = B * L
        self.NT = self.n_lat + self.n_ctx
        for tm in (1024, 512, 256):
            if T % tm == 0 and self.n_ctx % tm == 0:
                self.TM = tm
                break
        else:
            raise ValueError("unsupported sequence lengths")
        assert L == RET_CHUNK == HGRN_BLOCK and T % L == 0 and T % GRID_W == 0
        assert B + 1 <= MOD_ROWS
        self.tiles_per_batch = T // self.TM
        self.lat_tiles = self.n_lat // self.TM
        self.all_tiles = self.NT // self.TM

    def mod_row(self, i):
        return jnp.where(i < self.lat_tiles, i // self.tiles_per_batch, self.B)

    def pos_block(self, i):
        return jnp.where(i < self.lat_tiles, i % self.tiles_per_batch, self.tiles_per_batch)


def _rope_tables(plan, d, width):
    T = plan.T
    rows = T // GRID_W
    row = np.repeat(np.arange(rows, dtype=np.float32), GRID_W)
    col = np.tile(np.arange(GRID_W, dtype=np.float32), rows)
    nq = d // 4
    inv = (np.float32(ROPE_THETA) ** (-np.arange(nq, dtype=np.float32) * np.float32(2.0) / np.float32(d // 2)))
    inv = inv.astype(np.float32)
    ar, ac = row[:, None] * inv, col[:, None] * inv
    cr, sr, cc, sc = np.cos(ar), np.sin(ar), np.cos(ac), np.sin(ac)
    z = np.zeros_like(sr)
    c = np.concatenate([cr, cr, cc, cc], axis=1)
    s1 = np.concatenate([-sr, z, -sc, z], axis=1)
    s2 = np.concatenate([z, sr, z, sc], axis=1)
    if 2 * nq == LANES:
        s1, s2 = s1 + s2, None
    reps = width // d

    def finish(t, fill):
        t = np.tile(t, (1, reps))
        t = np.concatenate([t, np.full((plan.TM, width), fill, np.float32)], axis=0)
        return jnp.asarray(t.astype(np.float32))

    return finish(c, 1.0), finish(s1, 0.0), (None if s2 is None else finish(s2, 0.0))


def _mod_kernel(c_ref, w_ref, b_ref, o_ref):
    cond = _silu(c_ref[...]).astype(BF16)
    o_ref[...] = _dot(cond, w_ref[...].astype(BF16)) + b_ref[...]


def _mod_call(cstack, mod_w, mod_b):
    depth, D, _ = mod_w.shape
    out = pl.pallas_call(
        _mod_kernel,
        out_shape=jax.ShapeDtypeStruct((depth, MOD_ROWS, N_MOD * D), F32),
        grid=(depth, N_MOD),
        in_specs=[
            pl.BlockSpec((MOD_ROWS, D), lambda l, n: (0, 0)),
            pl.BlockSpec((None, D, D), lambda l, n: (l, 0, n)),
            pl.BlockSpec((None, 1, D), lambda l, n: (l, 0, n)),
        ],
        out_specs=pl.BlockSpec((None, MOD_ROWS, D), lambda l, n: (l, 0, n)),
        compiler_params=_params("arbitrary", "arbitrary"),
        name="mod_table",
    )(cstack, mod_w, mod_b.reshape(depth, 1, N_MOD * D))
    return out.reshape(depth, MOD_ROWS, N_MOD, D)


def _ffn_kernel(*refs, k0, F, lat_tiles, split_in, final):
    refs = list(refs)
    if split_in:
        hl_ref, hc_ref = refs[:2]
        x = jnp.where(pl.program_id(0) < lat_tiles, hl_ref[...], hc_ref[...])
        refs = refs[2:]
    else:
        x = refs[0][...]
        refs = refs[1:]
    mod_ref, w13_ref, w2_ref = refs[:3]
    o_ref = refs[-1]
    xn = _modulated(x, mod_ref, k0).astype(BF16)
    acc = None
    for lo in range(0, F, FFN_CHUNK):
        a = _dot(xn, w13_ref[:, lo:lo + FFN_CHUNK])
        b = _dot(xn, w13_ref[:, F + lo:F + lo + FFN_CHUNK])
        part = _dot((_silu(a) * b).astype(BF16), w2_ref[lo:lo + FFN_CHUNK, :])
        acc = part if acc is None else acc + part
    hn = x + 0.5 * mod_ref[k0 + 2:k0 + 3, :] * acc
    if final:
        hn = _rms(hn) * refs[3][...]
    o_ref[...] = hn


def _resident(shape, index_map):
    return pl.BlockSpec(shape, index_map, pipeline_mode=pl.Buffered(1))


def _ffn_call(h_in, mod, layer, k0, w13, w2, n_tiles, plan, final_g=None):
    split_in = isinstance(h_in, tuple)
    D, F = w2.shape[2], w2.shape[1]
    TM = plan.TM
    lat = plan.lat_tiles
    final = final_g is not None
    if split_in:
        in_specs = [pl.BlockSpec((TM, D), lambda i: (jnp.minimum(i, lat - 1), 0)),
                    pl.BlockSpec((TM, D), lambda i: (jnp.maximum(i - lat, 0), 0))]
        args = list(h_in)
    else:
        in_specs = [pl.BlockSpec((TM, D), lambda i: (i, 0))]
        args = [h_in]
    in_specs += [
        pl.BlockSpec((None, None, N_MOD, D), lambda i: (layer, plan.mod_row(i), 0, 0)),
        _resident((None, D, 2 * F), lambda i: (layer, 0, 0)),
        _resident((None, F, D), lambda i: (layer, 0, 0)),
    ]
    args += [mod, w13, w2]
    if final:
        in_specs.append(pl.BlockSpec((1, D), lambda i: (0, 0)))
        args.append(final_g.reshape(1, D))
    aliases = {} if (final or split_in) else {0: 0}
    rows = n_tiles * TM if final else plan.NT
    return pl.pallas_call(
        functools.partial(_ffn_kernel, k0=k0, F=F, lat_tiles=lat, split_in=split_in, final=final),
        out_shape=jax.ShapeDtypeStruct((rows, D), F32),
        grid=(n_tiles,),
        in_specs=in_specs,
        out_specs=pl.BlockSpec((TM, D), lambda i: (i, 0)),
        input_output_aliases=aliases,
        compiler_params=_params("arbitrary"),
        name="ffn_final" if final else "ffn",
    )(*args)


PROJ_COLS = 256


def _proj_diff_kernel(h_ref, mod_ref, w_ref, c_ref, s1_ref, s2_ref, q_ref, k_ref, v_ref, *, D, d):
    xm = _modulated(h_ref[...], mod_ref, 3).astype(BF16)
    c, s1, s2 = c_ref[...], s1_ref[...], s2_ref[...]
    nq = d // 4
    for n in range(0, D, PROJ_COLS):
        yq = _dot(xm, w_ref[:, n:n + PROJ_COLS]) * (d ** -0.5 * LOG2E)
        yk = _dot(xm, w_ref[:, D + n:D + n + PROJ_COLS])
        for m in range(0, PROJ_COLS, LANES):
            q_ref[:, n + m:n + m + LANES] = _rope(yq[:, m:m + LANES], c, s1, s2, nq).astype(BF16)
            k_ref[:, n + m:n + m + LANES] = _rope(yk[:, m:m + LANES], c, s1, s2, nq).astype(BF16)
        v_ref[:, n:n + PROJ_COLS] = _dot(xm, w_ref[:, 2 * D + n:2 * D + n + PROJ_COLS]).astype(BF16)


def _proj_ret_kernel(h_ref, mod_ref, w_ref, c_ref, s_ref, q_ref, k_ref, v_ref, g_ref, *, D, dk):
    xm = _modulated(h_ref[...], mod_ref, 3).astype(BF16)
    c, s = c_ref[...], s_ref[...]
    nq = dk // 4
    for n in range(0, D, PROJ_COLS):
        yq = _dot(xm, w_ref[:, n:n + PROJ_COLS])
        yk = _dot(xm, w_ref[:, D + n:D + n + PROJ_COLS]) * (dk ** -0.5)
        for m in range(0, PROJ_COLS, LANES):
            cm, sm = c[:, m:m + LANES], s[:, m:m + LANES]
            q_ref[:, n + m:n + m + LANES] = _rope(yq[:, m:m + LANES], cm, sm, None, nq).astype(BF16)
            k_ref[:, n + m:n + m + LANES] = _rope(yk[:, m:m + LANES], cm, sm, None, nq).astype(BF16)
    for n in range(0, 2 * D, PROJ_COLS):
        v_ref[:, n:n + PROJ_COLS] = _dot(xm, w_ref[:, 2 * D + n:2 * D + n + PROJ_COLS]).astype(BF16)
        g_ref[:, n:n + PROJ_COLS] = _silu(_dot(xm, w_ref[:, 4 * D + n:4 * D + n + PROJ_COLS])).astype(BF16)


def _proj_hgrn_kernel(h_ref, mod_ref, w_ref, q_ref, i_ref, g_ref, zf_ref, zb_ref, *, D):
    xm = _modulated(h_ref[...], mod_ref, 3).astype(BF16)
    for n in range(0, D, PROJ_COLS):
        sl = slice(n, n + PROJ_COLS)
        q_ref[:, sl] = _silu(_dot(xm, w_ref[:, n:n + PROJ_COLS])).astype(BF16)
        i_ref[:, sl] = _dot(xm, w_ref[:, D + n:D + n + PROJ_COLS]).astype(BF16)
        g_ref[:, sl] = _silu(_dot(xm, w_ref[:, 2 * D + n:2 * D + n + PROJ_COLS])).astype(BF16)
        zf_ref[:, sl] = _dot(xm, w_ref[:, 3 * D + n:3 * D + n + PROJ_COLS])
        zb_ref[:, sl] = _dot(xm, w_ref[:, 4 * D + n:4 * D + n + PROJ_COLS])


def _proj_gqa_kernel(h_ref, mod_ref, w_ref, c_ref, s1_ref, s2_ref, qg_ref, kg_ref,
                     q_ref, k_ref, v_ref, *, D, d, kvw):
    xm = _modulated(h_ref[...], mod_ref, 3).astype(BF16)
    c, s1, s2 = c_ref[...], s1_ref[...], s2_ref[...]
    nq = d // 4
    qg = qg_ref[...] * (d ** -0.5 * LOG2E)
    kg = kg_ref[...]
    for n in range(0, D, PROJ_COLS):
        yq = _dot(xm, w_ref[:, n:n + PROJ_COLS])
        for m in range(0, PROJ_COLS, LANES):
            q_ref[:, n + m:n + m + LANES] = _rope(_rms(yq[:, m:m + LANES]) * qg, c, s1, s2, nq).astype(BF16)
    yk = _dot(xm, w_ref[:, D:D + kvw])
    for m in range(0, kvw, LANES):
        k_ref[:, m:m + LANES] = _rope(_rms(yk[:, m:m + LANES]) * kg, c, s1, s2, nq).astype(BF16)
    v_ref[...] = _dot(xm, w_ref[:, D + kvw:D + 2 * kvw]).astype(BF16)


def _proj_call(kernel, H, mod, layer, w, j, extra, extra_specs, outs, plan, name):
    NT, D = H.shape
    TM = plan.TM
    in_specs = [
        pl.BlockSpec((TM, D), lambda i: (i, 0)),
        pl.BlockSpec((None, None, N_MOD, D), lambda i: (layer, plan.mod_row(i), 0, 0)),
        _resident((None,) + w.shape[1:], lambda i: (j, 0, 0)),
    ] + extra_specs
    return pl.pallas_call(
        kernel,
        out_shape=[jax.ShapeDtypeStruct((NT, wd), dt) for wd, dt in outs],
        grid=(plan.all_tiles,),
        in_specs=in_specs,
        out_specs=[pl.BlockSpec((TM, wd), lambda i: (i, 0)) for wd, _ in outs],
        compiler_params=_params("arbitrary"),
        name=name,
    )(H, mod, w, *extra)


def _table_specs(plan, tables):
    return [pl.BlockSpec((plan.TM, t.shape[1]), lambda i: (plan.pos_block(i), 0)) for t in tables]


def _flash(q, sources):
    m = l = acc = None
    for k_ref, v_ref, start, size in sources:
        k = k_ref[start:start + size, :]
        v = v_ref[start:start + size, :]
        s = _dot_nt(q, k)
        tiles = [s[:, t:t + LANES] for t in range(0, size, LANES)]
        mx = functools.reduce(jnp.maximum, tiles)
        ms = jnp.broadcast_to(jnp.max(mx, axis=1, keepdims=True), mx.shape)
        if m is None:
            m = ms
            ps = [jnp.exp2(t - m) for t in tiles]
            l = functools.reduce(jnp.add, ps)
            acc = _dot(jnp.concatenate(ps, axis=1).astype(BF16), v)
        else:
            m_new = jnp.maximum(m, ms)
            alpha = jnp.exp2(m - m_new)
            ps = [jnp.exp2(t - m_new) for t in tiles]
            l = alpha * l + functools.reduce(jnp.add, ps)
            acc = alpha * acc + _dot(jnp.concatenate(ps, axis=1).astype(BF16), v)
            m = m_new
    return acc, jnp.sum(l, axis=1, keepdims=True)


def _kv_sources(kv_refs, L, T):
    if len(kv_refs) == 2:
        kc, vc = kv_refs
        return [(kc, vc, 0, L)]
    kc, vc, kl, vl = kv_refs
    chunk = min(ATTN_KV_CHUNK, T)
    return [(kc, vc, 0, L)] + [(kl, vl, s, chunk) for s in range(0, T, chunk)]


def _diff_attn_kernel(*refs, lam_init, L, T):
    lam_ref, g_ref, q_ref = refs[:3]
    kv_refs, o_ref = refs[3:-1], refs[-1]
    q = q_ref[...]
    tq = q.shape[0]
    lane = lax.broadcasted_iota(jnp.int32, q.shape, 1)
    zero = jnp.zeros_like(q)
    half = LANES // 2
    qs = jnp.concatenate([jnp.where(lane < half, q, zero), jnp.where(lane >= half, q, zero)], axis=0)
    acc, l = _flash(qs, _kv_sources(kv_refs, L, T))
    lam = lam_ref[...]
    lam_full = (jnp.exp(jnp.sum(lam[0:1] * lam[1:2], axis=1, keepdims=True))
                - jnp.exp(jnp.sum(lam[2:3] * lam[3:4], axis=1, keepdims=True)) + lam_init)
    o = acc[:tq] / l[:tq] - lam_full * (acc[tq:] / l[tq:])
    o_ref[...] = (_rms(o) * g_ref[...] * (1.0 - lam_init)).astype(BF16)


def _gqa_attn_kernel(*refs, G, L, T):
    q_ref = refs[0]
    kv_refs, o_ref = refs[1:-1], refs[-1]
    tq = q_ref.shape[0]
    qs = jnp.concatenate([q_ref[:, g * LANES:(g + 1) * LANES] for g in range(G)], axis=0)
    acc, l = _flash(qs, _kv_sources(kv_refs, L, T))
    o = acc / l
    for g in range(G):
        o_ref[:, g * LANES:(g + 1) * LANES] = o[g * tq:(g + 1) * tq].astype(BF16)


def _attn_call(kernel, plan, q, k, v, kv_col, n_heads, q_width, small, *, tq_lat, ctx_out=None, name):
    B, T, L = plan.B, plan.T, plan.L
    NT = q.shape[0]
    ctx_row = plan.n_lat // L
    kvw = LANES
    small_specs = [pl.BlockSpec(s.shape, lambda b, h, i: (0, 0)) for s in small]
    kv_specs = [pl.BlockSpec((L, kvw), lambda b, h, i: (ctx_row + b, kv_col(h)))] * 2
    args = list(small) + [q, k, v]
    if ctx_out is None:
        tq = min(tq_lat, T)
        nq = T // tq
        q_spec = pl.BlockSpec((tq, q_width), lambda b, h, i: (b * nq + i, h))
        kv_specs = kv_specs + [pl.BlockSpec((T, kvw), lambda b, h, i: (b, kv_col(h)))] * 2
        args += [k, v]
        aliases = {}
    else:
        tq, nq = L, 1
        q_spec = pl.BlockSpec((tq, q_width), lambda b, h, i: (ctx_row + b, h))
        args.append(ctx_out)
        kv_specs = kv_specs + [pl.BlockSpec(memory_space=pl.ANY)]
        aliases = {len(args) - 1: 0}
    if ctx_out is not None:
        body = lambda *r: kernel(*r[:-2], r[-1])
    else:
        body = kernel
    return pl.pallas_call(
        body,
        out_shape=jax.ShapeDtypeStruct((NT, q.shape[1]), BF16),
        grid=(B, n_heads, nq),
        in_specs=small_specs + [q_spec] + kv_specs,
        out_specs=q_spec,
        input_output_aliases=aliases,
        compiler_params=_params("arbitrary", "arbitrary", "arbitrary"),
        name=name,
    )(*args)


def _ret_kernel(dec_ref, qc_ref, kc_ref, vc_ref, gc_ref, ql_ref, kl_ref, vl_ref, gl_ref,
                yc_ref, yl_ref, sb_ref, st_ref, *, C, n_chunks):
    lg = jnp.log(1.0 - jnp.exp(-dec_ref[...] * math.log(2.0)))
    lgf, lgb = lg[0:1], lg[1:2]
    i = lax.broadcasted_iota(jnp.int32, (C, 1), 0).astype(F32)
    qdf, kef = jnp.exp((i + 1.0) * lgf), jnp.exp((C - 1.0 - i) * lgf)
    qdb, keb = jnp.exp((C - i) * lgb), jnp.exp(i * lgb)
    gfc, gbc = jnp.exp(C * lgf), jnp.exp(C * lgb)
    dist = (lax.broadcasted_iota(jnp.int32, (C, C), 0) - lax.broadcasted_iota(jnp.int32, (C, C), 1)).astype(F32)
    w = jnp.where(dist > 0, jnp.exp(jnp.maximum(dist, 0.0) * lgf),
                  jnp.where(dist < 0, jnp.exp(jnp.maximum(-dist, 0.0) * lgb), 2.0))

    def intra(q, k, v):
        return _dot((_dot_nt(q, k) * w).astype(BF16), v)

    def kv_state(k, v, ke):
        return _dot_tn((k.astype(F32) * ke).astype(BF16), v)

    def readout(o, g):
        return (g.astype(F32) * _rms(o)).astype(BF16)

    qx, kx, vx = qc_ref[...], kc_ref[...], vc_ref[...]
    yc_ref[...] = readout(intra(qx, kx, vx), gc_ref[...])

    st_ref[...] = kv_state(kx, vx, keb)

    def bwd(t, carry):
        c = n_chunks - 1 - t
        rows = pl.ds(pl.multiple_of(c * C, C), C)
        s = st_ref[...]
        sb_ref[c] = s.astype(BF16)
        st_ref[...] = gbc * s + kv_state(kl_ref[rows, :], vl_ref[rows, :], keb)
        return carry

    lax.fori_loop(0, n_chunks, bwd, 0)

    st_ref[...] = kv_state(kx, vx, kef)

    def fwd(c, carry):
        rows = pl.ds(pl.multiple_of(c * C, C), C)
        q, k, v = ql_ref[rows, :], kl_ref[rows, :], vl_ref[rows, :]
        qf = q.astype(F32)
        s = st_ref[...]
        o = (intra(q, k, v) + _dot((qf * qdf).astype(BF16), s.astype(BF16))
             + _dot((qf * qdb).astype(BF16), sb_ref[c]))
        yl_ref[rows, :] = readout(o, gl_ref[rows, :])
        st_ref[...] = gfc * s + kv_state(k, v, kef)
        return carry

    lax.fori_loop(0, n_chunks, fwd, 0)


def _ret_call(plan, q, k, v, g, decay_exp):
    B, T, L = plan.B, plan.T, plan.L
    NT, D = q.shape
    H = RET_HEADS
    dk, dv = D // H, v.shape[1] // H
    C = RET_CHUNK
    n_chunks = T // C
    ctx_row = plan.n_lat // L
    dec = jnp.transpose(decay_exp.astype(F32)).reshape(H, 2, 1)
    ctx = lambda w: pl.BlockSpec((L, w), lambda b, h: (ctx_row + b, h))
    lat = lambda w: pl.BlockSpec((T, w), lambda b, h: (b, h))
    yc, yl = pl.pallas_call(
        functools.partial(_ret_kernel, C=C, n_chunks=n_chunks),
        out_shape=[jax.ShapeDtypeStruct((plan.n_ctx, v.shape[1]), BF16),
                   jax.ShapeDtypeStruct((plan.n_lat, v.shape[1]), BF16)],
        grid=(B, H),
        in_specs=[pl.BlockSpec((None, 2, 1), lambda b, h: (h, 0, 0)),
                  ctx(dk), ctx(dk), ctx(dv), ctx(dv), lat(dk), lat(dk), lat(dv), lat(dv)],
        out_specs=[pl.BlockSpec((L, dv), lambda b, h: (b, h)),
                   pl.BlockSpec((T, dv), lambda b, h: (b, h))],
        scratch_shapes=[pltpu.VMEM((n_chunks, dk, dv), BF16), pltpu.VMEM((dk, dv), F32)],
        compiler_params=_params("arbitrary", "arbitrary"),
        name="retention",
    )(dec, q, k, v, g, q, k, v, g)
    return yl, yc


def _hgrn_lower_bound(logits, layer):
    e = jnp.exp(logits - jnp.max(logits, axis=0, keepdims=True))
    p = e / jnp.sum(e, axis=0, keepdims=True)
    return jnp.sum(p[0:layer + 1], axis=0, keepdims=True) - p[0:1]


def _hgrn_kernel(lb_ref, qf_ref, vf_ref, zf_ref, qb_ref, vb_ref, zb_ref, of_ref, ob_ref,
                 sf_ref, sb_ref, *, layer, C, n_heads):
    @pl.when(pl.program_id(1) == 0)
    def _():
        sf_ref[...] = jnp.zeros_like(sf_ref)
        sb_ref[...] = jnp.zeros_like(sb_ref)

    lb = _hgrn_lower_bound(lb_ref[...], layer)
    r_i = lax.broadcasted_iota(jnp.int32, (C, C), 0)
    c_i = lax.broadcasted_iota(jnp.int32, (C, C), 1)
    lower = r_i >= c_i
    upper = r_i <= c_i
    n_sub = qf_ref.shape[0] // C

    def chunk(q_ref, v_ref, z_ref, o_ref, st_ref, rows, mask, mid, last):
        q, v, z = q_ref[rows, :].astype(F32), v_ref[rows, :], z_ref[rows, :]
        f = lb + (1.0 - lb) * jax.nn.sigmoid(z)
        kk = 1.0 - f
        b = jnp.dot(mask.astype(F32), jnp.log(f), preferred_element_type=F32,
                    precision=lax.Precision.HIGHEST)
        ref, tot = b[mid:mid + 1], b[last:last + 1]
        qd, kd = (q * jnp.exp(b - ref)).astype(BF16), (kk * jnp.exp(ref - b)).astype(BF16)
        qs, ke = (q * jnp.exp(b)).astype(BF16), (kk * jnp.exp(tot - b)).astype(BF16)
        dec = jnp.exp(tot)
        for h in range(n_heads):
            sl = slice(h * HGRN_HEAD_DIM, (h + 1) * HGRN_HEAD_DIM)
            att = jnp.where(mask, _dot_nt(qd[:, sl], kd[:, sl]), 0.0).astype(BF16)
            st = st_ref[h]
            o_ref[rows, sl] = _dot(att, v[:, sl]) + _dot_nt(qs[:, sl], st.astype(BF16))
            st_ref[h] = st * dec[:, sl] + _dot_tn(v[:, sl], ke[:, sl])

    for j in range(n_sub):
        chunk(qf_ref, vf_ref, zf_ref, of_ref, sf_ref, slice(j * C, (j + 1) * C), lower, C // 2 - 1, C - 1)
        jb = n_sub - 1 - j
        chunk(qb_ref, vb_ref, zb_ref, ob_ref, sb_ref, slice(jb * C, (jb + 1) * C), upper, C // 2, 0)


def _hgrn_call(plan, layer, lb_logits, q, v, zf, zb):
    B, T, L = plan.B, plan.T, plan.L
    NT, D = q.shape
    TB = HGRN_BLOCK
    nb = T // TB
    ctx_blk = plan.n_lat // TB
    H = D // HGRN_HEAD_DIM
    fwd = lambda b, s: (jnp.where(s == 0, ctx_blk + b, b * nb + s - 1), 0)
    bwd = lambda b, s: (jnp.where(s == 0, ctx_blk + b, b * nb + nb - s), 0)
    blk = lambda im: pl.BlockSpec((TB, D), im)
    return pl.pallas_call(
        functools.partial(_hgrn_kernel, layer=layer, C=HGRN_CHUNK, n_heads=H),
        out_shape=[jax.ShapeDtypeStruct((NT, D), F32), jax.ShapeDtypeStruct((NT, D), F32)],
        grid=(B, nb + 1),
        in_specs=[pl.BlockSpec(lb_logits.shape, lambda b, s: (0, 0)),
                  blk(fwd), blk(fwd), blk(fwd), blk(bwd), blk(bwd), blk(bwd)],
        out_specs=[blk(fwd), blk(bwd)],
        scratch_shapes=[pltpu.VMEM((H, HGRN_HEAD_DIM, HGRN_HEAD_DIM), F32),
                        pltpu.VMEM((H, HGRN_HEAD_DIM, HGRN_HEAD_DIM), F32)],
        compiler_params=_params("arbitrary", "arbitrary"),
        name="hgrn2",
    )(lb_logits, q, v, zf, q, v, zb)


def _oproj_kernel(h_ref, mod_ref, y_ref, w_ref, o_ref):
    o_ref[...] = h_ref[...] + mod_ref[5:6, :] * _dot(y_ref[...], w_ref[...])


def _oproj_split_kernel(h_ref, mod_ref, yl_ref, yc_ref, w_ref, o_ref, *, lat_tiles):
    i = pl.program_id(0)

    @pl.when(i < lat_tiles)
    def _():
        o_ref[...] = h_ref[...] + mod_ref[5:6, :] * _dot(yl_ref[...], w_ref[...])

    @pl.when(i >= lat_tiles)
    def _():
        o_ref[...] = h_ref[...] + mod_ref[5:6, :] * _dot(yc_ref[...], w_ref[...])


def _oproj_hgrn_kernel(h_ref, mod_ref, of_ref, ob_ref, g_ref, ng_ref, w_ref, o_ref):
    o = of_ref[...] + ob_ref[...]
    ng = ng_ref[...]
    ys = []
    for n in range(0, o.shape[1], HGRN_HEAD_DIM):
        sl = slice(n, n + HGRN_HEAD_DIM)
        ys.append((_rms(o[:, sl]) * ng * g_ref[:, sl].astype(F32)).astype(BF16))
    y = jnp.concatenate(ys, axis=1)
    o_ref[...] = h_ref[...] + mod_ref[5:6, :] * _dot(y, w_ref[...])


def _oproj_call(kernel, H, mod, layer, ys, small, w, j, n_tiles, plan, name, y_specs=None):
    NT, D = H.shape
    TM = plan.TM
    if y_specs is None:
        y_specs = [pl.BlockSpec((TM, y.shape[1]), lambda i: (i, 0)) for y in ys]
    in_specs = ([pl.BlockSpec((TM, D), lambda i: (i, 0)),
                 pl.BlockSpec((None, None, N_MOD, D), lambda i: (layer, plan.mod_row(i), 0, 0))]
                + y_specs
                + [pl.BlockSpec(s.shape, lambda i: (0, 0)) for s in small]
                + [_resident((None,) + w.shape[1:], lambda i: (j, 0, 0))])
    return pl.pallas_call(
        kernel,
        out_shape=jax.ShapeDtypeStruct((NT, D), F32),
        grid=(n_tiles,),
        in_specs=in_specs,
        out_specs=pl.BlockSpec((TM, D), lambda i: (i, 0)),
        input_output_aliases={0: 0},
        compiler_params=_params("arbitrary"),
        name=name,
    )(H, mod, *ys, *small, w)


def kernel(x, c, ctx, c_ctx, mod_w, mod_b, ffn1_w13, ffn1_w2, ffn2_w13, ffn2_w2, diff_w_in, diff_w_out, diff_lambda, diff_subln_g, ret_w_in, ret_w_out, ret_decay_exp, hgrn_w_in, hgrn_w_out, hgrn_lb_logits, hgrn_norm_g, gqa_w_in, gqa_w_out, gqa_q_norm_g, gqa_k_norm_g, final_norm_g):
    B, T, D = x.shape
    L = ctx.shape[1]
    depth = mod_w.shape[0]
    plan = _Plan(B, T, L)

    cstack = jnp.concatenate([c, c_ctx[None, :], jnp.zeros((MOD_ROWS - B - 1, D), F32)], axis=0)
    mod = _mod_call(cstack, mod_w, mod_b)
    H = (x.reshape(B * T, D), ctx.reshape(B * L, D))
    bf = lambda w: w.astype(BF16)
    ffn1_w13, ffn1_w2, ffn2_w13, ffn2_w2 = bf(ffn1_w13), bf(ffn1_w2), bf(ffn2_w13), bf(ffn2_w2)
    lat = plan.lat_tiles

    for i in range(depth):
        kind, j = i % N_MIXERS, i // N_MIXERS
        need_ctx = i < depth - 1
        mix_tiles = plan.all_tiles if need_ctx else plan.lat_tiles

        H = _ffn_call(H, mod, i, 0, ffn1_w13, ffn1_w2, plan.all_tiles, plan)

        if kind == 0:
            d = D // DIFF_HEADS // 2
            lam_init = 0.8 - 0.6 * math.exp(-0.3 * i)
            tabs = _rope_tables(plan, d, LANES)
            q, k, v = _proj_call(
                functools.partial(_proj_diff_kernel, D=D, d=d), H, mod, i, bf(diff_w_in), j,
                tabs, _table_specs(plan, tabs), [(D, BF16)] * 3, plan, "proj_diff")
            small = [diff_lambda[j], diff_subln_g[j].reshape(1, 2 * d)]
            kern = functools.partial(_diff_attn_kernel, lam_init=lam_init, L=L, T=T)
            y = _attn_call(kern, plan, q, k, v, lambda h: h, DIFF_HEADS, LANES, small,
                           tq_lat=DIFF_TQ, name="diff_attn")
            if need_ctx:
                y = _attn_call(kern, plan, q, k, v, lambda h: h, DIFF_HEADS, LANES, small,
                               tq_lat=DIFF_TQ, ctx_out=y, name="diff_attn_ctx")
            H = _oproj_call(_oproj_kernel, H, mod, i, [y], [], bf(diff_w_out), j, mix_tiles, plan, "oproj")
        elif kind == 1:
            dk = D // RET_HEADS
            c_t, s_t, _ = _rope_tables(plan, dk, dk)
            tabs = [c_t, s_t]
            q, k, v, g = _proj_call(
                functools.partial(_proj_ret_kernel, D=D, dk=dk), H, mod, i, bf(ret_w_in), j,
                tabs, _table_specs(plan, tabs), [(D, BF16), (D, BF16), (2 * D, BF16), (2 * D, BF16)],
                plan, "proj_ret")
            yl, yc = _ret_call(plan, q, k, v, g, ret_decay_exp[j])
            y_specs = [pl.BlockSpec((plan.TM, 2 * D), lambda t: (jnp.minimum(t, lat - 1), 0)),
                       pl.BlockSpec((plan.TM, 2 * D), lambda t: (jnp.maximum(t - lat, 0), 0))]
            H = _oproj_call(functools.partial(_oproj_split_kernel, lat_tiles=lat), H, mod, i, [yl, yc], [],
                            bf(ret_w_out), j, mix_tiles, plan, "oproj_ret", y_specs=y_specs)
        elif kind == 2:
            q, v, g, zf, zb = _proj_call(
                functools.partial(_proj_hgrn_kernel, D=D), H, mod, i, bf(hgrn_w_in), j,
                [], [], [(D, BF16), (D, BF16), (D, BF16), (D, F32), (D, F32)], plan, "proj_hgrn")
            o_f, o_b = _hgrn_call(plan, i, hgrn_lb_logits, q, v, zf, zb)
            H = _oproj_call(_oproj_hgrn_kernel, H, mod, i, [o_f, o_b, g],
                            [hgrn_norm_g[j].reshape(1, HGRN_HEAD_DIM)], bf(hgrn_w_out), j,
                            mix_tiles, plan, "oproj_hgrn")
        else:
            d = GQA_HEAD_DIM
            kvw = GQA_KV_HEADS * d
            G = D // d // GQA_KV_HEADS
            tabs = _rope_tables(plan, d, LANES)
            extra = list(tabs) + [gqa_q_norm_g[j].reshape(1, d), gqa_k_norm_g[j].reshape(1, d)]
            specs = _table_specs(plan, tabs) + [pl.BlockSpec((1, d), lambda t: (0, 0))] * 2
            q, k, v = _proj_call(
                functools.partial(_proj_gqa_kernel, D=D, d=d, kvw=kvw), H, mod, i, bf(gqa_w_in), j,
                extra, specs, [(D, BF16), (kvw, BF16), (kvw, BF16)], plan, "proj_gqa")
            kern = functools.partial(_gqa_attn_kernel, G=G, L=L, T=T)
            y = _attn_call(kern, plan, q, k, v, lambda h: h, GQA_KV_HEADS, G * d, [],
                           tq_lat=GQA_TQ, name="gqa_attn")
            if need_ctx:
                y = _attn_call(kern, plan, q, k, v, lambda h: h, GQA_KV_HEADS, G * d, [],
                               tq_lat=GQA_TQ, ctx_out=y, name="gqa_attn_ctx")
            H = _oproj_call(_oproj_kernel, H, mod, i, [y], [], bf(gqa_w_out), j, mix_tiles, plan, "oproj")

        if need_ctx:
            H = _ffn_call(H, mod, i, 6, ffn2_w13, ffn2_w2, plan.all_tiles, plan)
        else:
            H = _ffn_call(H, mod, i, 6, ffn2_w13, ffn2_w2, plan.lat_tiles, plan, final_g=final_norm_g)
    return H.reshape(B, T, D)
```

```python
import functools
import math

import jax
import jax.numpy as jnp
import numpy as np
from jax import lax
from jax.experimental import pallas as pl
from jax.experimental.pallas import tpu as pltpu

F32 = jnp.float32
BF16 = jnp.bfloat16

NORM_EPS = 1e-6
ROPE_THETA = 10000.0
GRID_W = 64
N_MOD = 9
N_MIXERS = 4
LOG2E = math.log2(math.e)

LANES = 128
MOD_ROWS = 8
VMEM_LIMIT = 56 * 1024 * 1024

DIFF_HEADS = 8
RET_HEADS = 4
GQA_KV_HEADS = 2
GQA_HEAD_DIM = 128
HGRN_HEAD_DIM = 128
FFN_CHUNK = 256
RET_CHUNK = 256
RET_UNROLL = 4
HGRN_CHUNK = 128
HGRN_BLOCK = 256
ATTN_KV_CHUNK = 512
DIFF_TQ = 1024
GQA_TQ = 512


def _params(*sem):
    return pltpu.CompilerParams(dimension_semantics=sem, vmem_limit_bytes=VMEM_LIMIT)


def _dot(a, b):
    return jnp.dot(a, b, preferred_element_type=F32)


def _dot_nt(a, b):
    return lax.dot_general(a, b, (((1,), (1,)), ((), ())), preferred_element_type=F32)


def _dot_tn(a, b):
    return lax.dot_general(a, b, (((0,), (0,)), ((), ())), preferred_element_type=F32)


def _silu(x):
    return x * jax.nn.sigmoid(x)


def _rms(x):
    return x * lax.rsqrt(jnp.mean(x * x, axis=-1, keepdims=True) + NORM_EPS)


def _rms_block(y):
    sq = y * y
    hi = sq.astype(BF16)
    lo = (sq - hi.astype(F32)).astype(BF16)
    ones = jnp.ones((LANES, LANES), BF16)
    ms = (_dot(hi, ones) + _dot(lo, ones)) * (1.0 / LANES)
    return y * lax.rsqrt(ms + NORM_EPS)


def _modulated(h, mod_ref, k0):
    shift = mod_ref[k0:k0 + 1, :]
    scale = mod_ref[k0 + 1:k0 + 2, :]
    return _rms(h) * (1.0 + scale) + shift


HALF = LANES // 2


def _rope(y, c, s):
    return y * c + pltpu.roll(y, HALF, 1) * s


def _pair_halves(w, d):
    nq = d // 4
    lead = w.shape[:-1]
    x = w.reshape(*lead, -1, LANES // d, 2, 2, nq)
    return jnp.moveaxis(x, -2, -4).reshape(*lead, -1)


class _Plan:
    def __init__(self, B, T, L):
        self.B, self.T, self.L = B, T, L
        self.n_lat = B * T
        self.n_ctx = B * L
        self.NT = self.n_lat + self.n_ctx
        for tm in (1024, 512, 256):
            if T % tm == 0 and self.n_ctx % tm == 0:
                self.TM = tm
                break
        else:
            raise ValueError("unsupported sequence lengths")
        assert L == RET_CHUNK == HGRN_BLOCK and T % L == 0 and T % GRID_W == 0
        assert B + 1 <= MOD_ROWS
        self.tiles_per_batch = T // self.TM
        self.lat_tiles = self.n_lat // self.TM
        self.all_tiles = self.NT // self.TM

    def mod_row(self, i):
        return jnp.where(i < self.lat_tiles, i // self.tiles_per_batch, self.B)

    def pos_block(self, i):
        return jnp.where(i < self.lat_tiles, i % self.tiles_per_batch, self.tiles_per_batch)


def _rope_tables(plan, d, width):
    T = plan.T
    rows = T // GRID_W
    row = np.repeat(np.arange(rows, dtype=np.float32), GRID_W)
    col = np.tile(np.arange(GRID_W, dtype=np.float32), rows)
    nq = d // 4
    inv = (np.float32(ROPE_THETA) ** (-np.arange(nq, dtype=np.float32) * np.float32(2.0) / np.float32(d // 2)))
    inv = inv.astype(np.float32)
    ar, ac = row[:, None] * inv, col[:, None] * inv
    cr, sr, cc, sc = np.cos(ar), np.sin(ar), np.cos(ac), np.sin(ac)
    if 2 * nq == LANES:
        c = np.concatenate([cr, cr, cc, cc], axis=1)
        s = np.concatenate([-sr, sr, -sc, sc], axis=1)
    else:
        units = LANES // d
        c = np.concatenate([cr, cc] * (2 * units), axis=1)
        s = np.concatenate([-sr, -sc] * units + [sr, sc] * units, axis=1)
    assert c.shape[1] == width

    def finish(t, fill):
        t = np.concatenate([t, np.full((plan.TM, width), fill, np.float32)], axis=0)
        return jnp.asarray(t.astype(np.float32))

    return finish(c, 1.0), finish(s, 0.0)


def _mod_kernel(c_ref, w_ref, b_ref, o_ref):
    cond = _silu(c_ref[...]).astype(BF16)
    o_ref[...] = _dot(cond, w_ref[...].astype(BF16)) + b_ref[...]


def _mod_call(cstack, mod_w, mod_b):
    depth, D, _ = mod_w.shape
    out = pl.pallas_call(
        _mod_kernel,
        out_shape=jax.ShapeDtypeStruct((depth, MOD_ROWS, N_MOD * D), F32),
        grid=(depth, N_MOD),
        in_specs=[
            pl.BlockSpec((MOD_ROWS, D), lambda l, n: (0, 0)),
            pl.BlockSpec((None, D, D), lambda l, n: (l, 0, n)),
            pl.BlockSpec((None, 1, D), lambda l, n: (l, 0, n)),
        ],
        out_specs=pl.BlockSpec((None, MOD_ROWS, D), lambda l, n: (l, 0, n)),
        compiler_params=_params("arbitrary", "arbitrary"),
        name="mod_table",
    )(cstack, mod_w, mod_b.reshape(depth, 1, N_MOD * D))
    return out.reshape(depth, MOD_ROWS, N_MOD, D)


def _ffn_kernel(*refs, k0, F, lat_tiles, split_in, final):
    refs = list(refs)
    if split_in:
        hl_ref, hc_ref = refs[:2]
        x = jnp.where(pl.program_id(0) < lat_tiles, hl_ref[...], hc_ref[...])
        refs = refs[2:]
    else:
        x = refs[0][...]
        refs = refs[1:]
    mod_ref, w13_ref, w2_ref = refs[:3]
    o_ref = refs[-1]
    xn = _modulated(x, mod_ref, k0).astype(BF16)
    acc = None
    for lo in range(0, F, FFN_CHUNK):
        a = _dot(xn, w13_ref[:, lo:lo + FFN_CHUNK])
        b = _dot(xn, w13_ref[:, F + lo:F + lo + FFN_CHUNK])
        part = _dot((_silu(a) * b).astype(BF16), w2_ref[lo:lo + FFN_CHUNK, :])
        acc = part if acc is None else acc + part
    hn = x + 0.5 * mod_ref[k0 + 2:k0 + 3, :] * acc
    if final:
        hn = _rms(hn) * refs[3][...]
    o_ref[...] = hn


def _resident(shape, index_map):
    return pl.BlockSpec(shape, index_map, pipeline_mode=pl.Buffered(1))


def _ffn_call(h_in, mod, layer, k0, w13, w2, n_tiles, plan, final_g=None):
    split_in = isinstance(h_in, tuple)
    D, F = w2.shape[2], w2.shape[1]
    TM = plan.TM
    lat = plan.lat_tiles
    final = final_g is not None
    if split_in:
        in_specs = [pl.BlockSpec((TM, D), lambda i: (jnp.minimum(i, lat - 1), 0)),
                    pl.BlockSpec((TM, D), lambda i: (jnp.maximum(i - lat, 0), 0))]
        args = list(h_in)
    else:
        in_specs = [pl.BlockSpec((TM, D), lambda i: (i, 0))]
        args = [h_in]
    in_specs += [
        pl.BlockSpec((None, None, N_MOD, D), lambda i: (layer, plan.mod_row(i), 0, 0)),
        _resident((None, D, 2 * F), lambda i: (layer, 0, 0)),
        _resident((None, F, D), lambda i: (layer, 0, 0)),
    ]
    args += [mod, w13, w2]
    if final:
        in_specs.append(pl.BlockSpec((1, D), lambda i: (0, 0)))
        args.append(final_g.reshape(1, D))
    aliases = {} if (final or split_in) else {0: 0}
    rows = n_tiles * TM if final else plan.NT
    return pl.pallas_call(
        functools.partial(_ffn_kernel, k0=k0, F=F, lat_tiles=lat, split_in=split_in, final=final),
        out_shape=jax.ShapeDtypeStruct((rows, D), F32),
        grid=(n_tiles,),
        in_specs=in_specs,
        out_specs=pl.BlockSpec((TM, D), lambda i: (i, 0)),
        input_output_aliases=aliases,
        compiler_params=_params("arbitrary"),
        name="ffn_final" if final else "ffn",
    )(*args)


PROJ_COLS = 256


def _proj_diff_kernel(h_ref, mod_ref, w_ref, c_ref, s_ref, q_ref, k_ref, v_ref, *, D, d):
    xm = _modulated(h_ref[...], mod_ref, 3).astype(BF16)
    c, s = c_ref[...], s_ref[...]
    for n in range(0, D, PROJ_COLS):
        yq = _dot(xm, w_ref[:, n:n + PROJ_COLS]) * (d ** -0.5 * LOG2E)
        yk = _dot(xm, w_ref[:, D + n:D + n + PROJ_COLS])
        for m in range(0, PROJ_COLS, LANES):
            q_ref[:, n + m:n + m + LANES] = _rope(yq[:, m:m + LANES], c, s).astype(BF16)
            k_ref[:, n + m:n + m + LANES] = _rope(yk[:, m:m + LANES], c, s).astype(BF16)
        v_ref[:, n:n + PROJ_COLS] = _dot(xm, w_ref[:, 2 * D + n:2 * D + n + PROJ_COLS]).astype(BF16)


def _proj_ret_kernel(h_ref, mod_ref, w_ref, c_ref, s_ref, q_ref, k_ref, v_ref, g_ref, *, D, dk):
    xm = _modulated(h_ref[...], mod_ref, 3).astype(BF16)
    c, s = c_ref[...], s_ref[...]
    for n in range(0, D, PROJ_COLS):
        yq = _dot(xm, w_ref[:, n:n + PROJ_COLS])
        yk = _dot(xm, w_ref[:, D + n:D + n + PROJ_COLS]) * (dk ** -0.5)
        for m in range(0, PROJ_COLS, LANES):
            cm, sm = c[:, m:m + LANES], s[:, m:m + LANES]
            q_ref[:, n + m:n + m + LANES] = _rope(yq[:, m:m + LANES], cm, sm).astype(BF16)
            k_ref[:, n + m:n + m + LANES] = _rope(yk[:, m:m + LANES], cm, sm).astype(BF16)
    for n in range(0, 2 * D, PROJ_COLS):
        v_ref[:, n:n + PROJ_COLS] = _dot(xm, w_ref[:, 2 * D + n:2 * D + n + PROJ_COLS]).astype(BF16)
        g_ref[:, n:n + PROJ_COLS] = _silu(_dot(xm, w_ref[:, 4 * D + n:4 * D + n + PROJ_COLS])).astype(BF16)


def _proj_hgrn_kernel(h_ref, mod_ref, w_ref, q_ref, i_ref, g_ref, zf_ref, zb_ref, *, D):
    xm = _modulated(h_ref[...], mod_ref, 3).astype(BF16)
    for n in range(0, D, PROJ_COLS):
        sl = slice(n, n + PROJ_COLS)
        q_ref[:, sl] = _silu(_dot(xm, w_ref[:, n:n + PROJ_COLS])).astype(BF16)
        i_ref[:, sl] = _dot(xm, w_ref[:, D + n:D + n + PROJ_COLS]).astype(BF16)
        g_ref[:, sl] = _silu(_dot(xm, w_ref[:, 2 * D + n:2 * D + n + PROJ_COLS])).astype(BF16)
        zf_ref[:, sl] = _dot(xm, w_ref[:, 3 * D + n:3 * D + n + PROJ_COLS])
        zb_ref[:, sl] = _dot(xm, w_ref[:, 4 * D + n:4 * D + n + PROJ_COLS])


def _proj_gqa_kernel(h_ref, mod_ref, w_ref, c_ref, s_ref, qg_ref, kg_ref,
                     q_ref, k_ref, v_ref, *, D, d, kvw):
    xm = _modulated(h_ref[...], mod_ref, 3).astype(BF16)
    c, s = c_ref[...], s_ref[...]
    qg = qg_ref[...] * (d ** -0.5 * LOG2E)
    kg = kg_ref[...]
    for n in range(0, D, PROJ_COLS):
        yq = _dot(xm, w_ref[:, n:n + PROJ_COLS])
        for m in range(0, PROJ_COLS, LANES):
            q_ref[:, n + m:n + m + LANES] = _rope(_rms_block(yq[:, m:m + LANES]) * qg, c, s).astype(BF16)
    yk = _dot(xm, w_ref[:, D:D + kvw])
    for m in range(0, kvw, LANES):
        k_ref[:, m:m + LANES] = _rope(_rms_block(yk[:, m:m + LANES]) * kg, c, s).astype(BF16)
    v_ref[...] = _dot(xm, w_ref[:, D + kvw:D + 2 * kvw]).astype(BF16)


def _proj_call(kernel, H, mod, layer, w, j, extra, extra_specs, outs, plan, name):
    NT, D = H.shape
    TM = plan.TM
    in_specs = [
        pl.BlockSpec((TM, D), lambda i: (i, 0)),
        pl.BlockSpec((None, None, N_MOD, D), lambda i: (layer, plan.mod_row(i), 0, 0)),
        _resident((None,) + w.shape[1:], lambda i: (j, 0, 0)),
    ] + extra_specs
    return pl.pallas_call(
        kernel,
        out_shape=[jax.ShapeDtypeStruct((NT, wd), dt) for wd, dt in outs],
        grid=(plan.all_tiles,),
        in_specs=in_specs,
        out_specs=[pl.BlockSpec((TM, wd), lambda i: (i, 0)) for wd, _ in outs],
        compiler_params=_params("arbitrary"),
        name=name,
    )(H, mod, w, *extra)


def _table_specs(plan, tables):
    return [pl.BlockSpec((plan.TM, t.shape[1]), lambda i: (plan.pos_block(i), 0)) for t in tables]


def _flash(q, sources):
    m = l = acc = None
    for k_ref, v_ref, start, size in sources:
        k = k_ref[start:start + size, :]
        v = v_ref[start:start + size, :]
        s = _dot_nt(q, k)
        tiles = [s[:, t:t + LANES] for t in range(0, size, LANES)]
        mx = functools.reduce(jnp.maximum, tiles)
        ms = jnp.broadcast_to(jnp.max(mx, axis=1, keepdims=True), mx.shape)
        if m is None:
            m = ms
            ps = [jnp.exp2(t - m) for t in tiles]
            l = functools.reduce(jnp.add, ps)
            acc = _dot(jnp.concatenate(ps, axis=1).astype(BF16), v)
        else:
            m_new = jnp.maximum(m, ms)
            alpha = jnp.exp2(m - m_new)
            ps = [jnp.exp2(t - m_new) for t in tiles]
            l = alpha * l + functools.reduce(jnp.add, ps)
            acc = alpha * acc + _dot(jnp.concatenate(ps, axis=1).astype(BF16), v)
            m = m_new
    return acc, jnp.sum(l, axis=1, keepdims=True)


def _kv_sources(kv_refs, L, T):
    if len(kv_refs) == 2:
        kc, vc = kv_refs
        return [(kc, vc, 0, L)]
    kc, vc, kl, vl = kv_refs
    chunk = min(ATTN_KV_CHUNK, T)
    return [(kc, vc, 0, L)] + [(kl, vl, s, chunk) for s in range(0, T, chunk)]


def _diff_attn_kernel(*refs, lam_init, L, T):
    lam_ref, g_ref, q_ref = refs[:3]
    kv_refs, o_ref = refs[3:-1], refs[-1]
    q = q_ref[...]
    tq = q.shape[0]
    lane = lax.broadcasted_iota(jnp.int32, q.shape, 1)
    map0 = (lane & (HALF // 2)) == 0
    zero = jnp.zeros_like(q)
    qs = jnp.concatenate([jnp.where(map0, q, zero), jnp.where(map0, zero, q)], axis=0)
    acc, l = _flash(qs, _kv_sources(kv_refs, L, T))
    lam = lam_ref[...]
    lam_full = (jnp.exp(jnp.sum(lam[0:1] * lam[1:2], axis=1, keepdims=True))
                - jnp.exp(jnp.sum(lam[2:3] * lam[3:4], axis=1, keepdims=True)) + lam_init)
    o = acc[:tq] / l[:tq] - lam_full * (acc[tq:] / l[tq:])
    o_ref[...] = (_rms(o) * g_ref[...] * (1.0 - lam_init)).astype(BF16)


def _gqa_attn_kernel(*refs, G, L, T):
    q_ref = refs[0]
    kv_refs, o_ref = refs[1:-1], refs[-1]
    tq = q_ref.shape[0]
    qs = jnp.concatenate([q_ref[:, g * LANES:(g + 1) * LANES] for g in range(G)], axis=0)
    acc, l = _flash(qs, _kv_sources(kv_refs, L, T))
    o = acc / l
    for g in range(G):
        o_ref[:, g * LANES:(g + 1) * LANES] = o[g * tq:(g + 1) * tq].astype(BF16)


def _attn_call(kernel, plan, q, k, v, kv_col, n_heads, q_width, small, *, tq_lat, ctx_out=None, name):
    B, T, L = plan.B, plan.T, plan.L
    NT = q.shape[0]
    ctx_row = plan.n_lat // L
    kvw = LANES
    small_specs = [pl.BlockSpec(s.shape, lambda b, h, i: (0, 0)) for s in small]
    kv_specs = [pl.BlockSpec((L, kvw), lambda b, h, i: (ctx_row + b, kv_col(h)))] * 2
    args = list(small) + [q, k, v]
    if ctx_out is None:
        tq = min(tq_lat, T)
        nq = T // tq
        q_spec = pl.BlockSpec((tq, q_width), lambda b, h, i: (b * nq + i, h))
        kv_specs = kv_specs + [pl.BlockSpec((T, kvw), lambda b, h, i: (b, kv_col(h)))] * 2
        args += [k, v]
        aliases = {}
    else:
        tq, nq = L, 1
        q_spec = pl.BlockSpec((tq, q_width), lambda b, h, i: (ctx_row + b, h))
        args.append(ctx_out)
        kv_specs = kv_specs + [pl.BlockSpec(memory_space=pl.ANY)]
        aliases = {len(args) - 1: 0}
    if ctx_out is not None:
        body = lambda *r: kernel(*r[:-2], r[-1])
    else:
        body = kernel
    return pl.pallas_call(
        body,
        out_shape=jax.ShapeDtypeStruct((NT, q.shape[1]), BF16),
        grid=(B, n_heads, nq),
        in_specs=small_specs + [q_spec] + kv_specs,
        out_specs=q_spec,
        input_output_aliases=aliases,
        compiler_params=_params("arbitrary", "arbitrary", "arbitrary"),
        name=name,
    )(*args)


def _ret_kernel(dec_ref, qc_ref, kc_ref, vc_ref, gc_ref, ql_ref, kl_ref, vl_ref, gl_ref,
                yc_ref, yl_ref, sb_ref, st_ref, *, C, n_chunks):
    lg = jnp.log(1.0 - jnp.exp(-dec_ref[...] * math.log(2.0)))
    lgf, lgb = lg[0:1], lg[1:2]
    i = lax.broadcasted_iota(jnp.int32, (C, 1), 0).astype(F32)
    qdf, kef = jnp.exp((i + 1.0) * lgf), jnp.exp((C - 1.0 - i) * lgf)
    qdb, keb = jnp.exp((C - i) * lgb), jnp.exp(i * lgb)
    gfc, gbc = jnp.exp(C * lgf), jnp.exp(C * lgb)
    dist = (lax.broadcasted_iota(jnp.int32, (C, C), 0) - lax.broadcasted_iota(jnp.int32, (C, C), 1)).astype(F32)
    w = jnp.where(dist > 0, jnp.exp(jnp.maximum(dist, 0.0) * lgf),
                  jnp.where(dist < 0, jnp.exp(jnp.maximum(-dist, 0.0) * lgb), 2.0))

    def intra(q, k, v):
        return _dot((_dot_nt(q, k) * w).astype(BF16), v)

    def kv_state(k, v, ke):
        return _dot_tn((k.astype(F32) * ke).astype(BF16), v)

    def readout(o, g):
        return (g.astype(F32) * _rms(o)).astype(BF16)

    qx, kx, vx = qc_ref[...], kc_ref[...], vc_ref[...]
    yc_ref[...] = readout(intra(qx, kx, vx), gc_ref[...])

    st_ref[...] = kv_state(kx, vx, keb)

    def bwd(t, carry):
        c = n_chunks - 1 - t
        rows = pl.ds(pl.multiple_of(c * C, C), C)
        s = st_ref[...]
        sb_ref[c] = s.astype(BF16)
        st_ref[...] = gbc * s + kv_state(kl_ref[rows, :], vl_ref[rows, :], keb)
        return carry

    lax.fori_loop(0, n_chunks, bwd, 0, unroll=RET_UNROLL)

    st_ref[...] = kv_state(kx, vx, kef)

    def fwd(c, carry):
        rows = pl.ds(pl.multiple_of(c * C, C), C)
        q, k, v = ql_ref[rows, :], kl_ref[rows, :], vl_ref[rows, :]
        qf = q.astype(F32)
        s = st_ref[...]
        o = (intra(q, k, v) + _dot((qf * qdf).astype(BF16), s.astype(BF16))
             + _dot((qf * qdb).astype(BF16), sb_ref[c]))
        yl_ref[rows, :] = readout(o, gl_ref[rows, :])
        st_ref[...] = gfc * s + kv_state(k, v, kef)
        return carry

    lax.fori_loop(0, n_chunks, fwd, 0, unroll=RET_UNROLL)


def _ret_call(plan, q, k, v, g, decay_exp):
    B, T, L = plan.B, plan.T, plan.L
    NT, D = q.shape
    H = RET_HEADS
    dk, dv = D // H, v.shape[1] // H
    C = RET_CHUNK
    n_chunks = T // C
    ctx_row = plan.n_lat // L
    dec = jnp.transpose(decay_exp.astype(F32)).reshape(H, 2, 1)
    ctx = lambda w: pl.BlockSpec((L, w), lambda b, h: (ctx_row + b, h))
    lat = lambda w: pl.BlockSpec((T, w), lambda b, h: (b, h))
    yc, yl = pl.pallas_call(
        functools.partial(_ret_kernel, C=C, n_chunks=n_chunks),
        out_shape=[jax.ShapeDtypeStruct((plan.n_ctx, v.shape[1]), BF16),
                   jax.ShapeDtypeStruct((plan.n_lat, v.shape[1]), BF16)],
        grid=(B, H),
        in_specs=[pl.BlockSpec((None, 2, 1), lambda b, h: (h, 0, 0)),
                  ctx(dk), ctx(dk), ctx(dv), ctx(dv), lat(dk), lat(dk), lat(dv), lat(dv)],
        out_specs=[pl.BlockSpec((L, dv), lambda b, h: (b, h)),
                   pl.BlockSpec((T, dv), lambda b, h: (b, h))],
        scratch_shapes=[pltpu.VMEM((n_chunks, dk, dv), BF16), pltpu.VMEM((dk, dv), F32)],
        compiler_params=_params("arbitrary", "arbitrary"),
        name="retention",
    )(dec, q, k, v, g, q, k, v, g)
    return yl, yc


def _hgrn_lower_bound(logits, layer):
    e = jnp.exp(logits - jnp.max(logits, axis=0, keepdims=True))
    p = e / jnp.sum(e, axis=0, keepdims=True)
    return jnp.sum(p[0:layer + 1], axis=0, keepdims=True) - p[0:1]


def _hgrn_kernel(lb_ref, qf_ref, vf_ref, zf_ref, qb_ref, vb_ref, zb_ref, of_ref, ob_ref,
                 sf_ref, sb_ref, *, layer, C, n_heads):
    @pl.when(pl.program_id(1) == 0)
    def _():
        sf_ref[...] = jnp.zeros_like(sf_ref)
        sb_ref[...] = jnp.zeros_like(sb_ref)

    lb = _hgrn_lower_bound(lb_ref[...], layer)
    TB, D = qf_ref.shape
    n_sub = TB // C
    r_i = lax.broadcasted_iota(jnp.int32, (TB, TB), 0)
    c_i = lax.broadcasted_iota(jnp.int32, (TB, TB), 1)
    same_chunk = (r_i // C) == (c_i // C)
    m_r = lax.broadcasted_iota(jnp.int32, (C, C), 0)
    m_c = lax.broadcasted_iota(jnp.int32, (C, C), 1)

    def direction(q_ref, v_ref, z_ref, o_ref, st_ref, causal, mask, mid, last, order):
        f = lb + (1.0 - lb) * jax.nn.sigmoid(z_ref[...])
        kk = 1.0 - f
        la = jnp.log(f)
        tri = jnp.where(same_chunk & causal, 1.0, 0.0).astype(BF16)
        hi = la.astype(BF16)
        r1 = la - hi.astype(F32)
        md = r1.astype(BF16)
        lo = (r1 - md.astype(F32)).astype(BF16)
        b = _dot(tri, hi) + _dot(tri, md) + _dot(tri, lo)
        refs = [b[j * C + mid:j * C + mid + 1] for j in range(n_sub)]
        tots = [b[j * C + last:j * C + last + 1] for j in range(n_sub)]
        ref_rows = jnp.concatenate([jnp.broadcast_to(r, (C, D)) for r in refs], axis=0)
        dl = (b - ref_rows) * LOG2E
        qd = q_ref[...].astype(F32) * jnp.exp2(dl)
        kd = kk * jnp.exp2(-dl)
        v = v_ref[...]
        for j in order:
            rows = slice(j * C, (j + 1) * C)
            qd_j, kd_j = qd[rows], kd[rows]
            qdb, kdb = qd_j.astype(BF16), kd_j.astype(BF16)
            qs = (qd_j * jnp.exp(refs[j])).astype(BF16)
            ke = (kd_j * jnp.exp(tots[j] - refs[j])).astype(BF16)
            dec = jnp.exp(tots[j])
            for h in range(n_heads):
                sl = slice(h * HGRN_HEAD_DIM, (h + 1) * HGRN_HEAD_DIM)
                att = jnp.where(mask, _dot_nt(qdb[:, sl], kdb[:, sl]), 0.0).astype(BF16)
                st = st_ref[h]
                o_ref[rows, sl] = _dot(att, v[rows, sl]) + _dot_nt(qs[:, sl], st.astype(BF16))
                st_ref[h] = st * dec[:, sl] + _dot_tn(v[rows, sl], ke[:, sl])

    direction(qf_ref, vf_ref, zf_ref, of_ref, sf_ref, r_i >= c_i, m_r >= m_c, C // 2 - 1, C - 1,
              range(n_sub))
    direction(qb_ref, vb_ref, zb_ref, ob_ref, sb_ref, r_i <= c_i, m_r <= m_c, C // 2, 0,
              range(n_sub - 1, -1, -1))


def _hgrn_call(plan, layer, lb_logits, q, v, zf, zb):
    B, T, L = plan.B, plan.T, plan.L
    NT, D = q.shape
    TB = HGRN_BLOCK
    nb = T // TB
    ctx_blk = plan.n_lat // TB
    H = D // HGRN_HEAD_DIM
    fwd = lambda b, s: (jnp.where(s == 0, ctx_blk + b, b * nb + s - 1), 0)
    bwd = lambda b, s: (jnp.where(s == 0, ctx_blk + b, b * nb + nb - s), 0)
    blk = lambda im: pl.BlockSpec((TB, D), im)
    return pl.pallas_call(
        functools.partial(_hgrn_kernel, layer=layer, C=HGRN_CHUNK, n_heads=H),
        out_shape=[jax.ShapeDtypeStruct((NT, D), F32), jax.ShapeDtypeStruct((NT, D), F32)],
        grid=(B, nb + 1),
        in_specs=[pl.BlockSpec(lb_logits.shape, lambda b, s: (0, 0)),
                  blk(fwd), blk(fwd), blk(fwd), blk(bwd), blk(bwd), blk(bwd)],
        out_specs=[blk(fwd), blk(bwd)],
        scratch_shapes=[pltpu.VMEM((H, HGRN_HEAD_DIM, HGRN_HEAD_DIM), F32),
                        pltpu.VMEM((H, HGRN_HEAD_DIM, HGRN_HEAD_DIM), F32)],
        compiler_params=_params("arbitrary", "arbitrary"),
        name="hgrn2",
    )(lb_logits, q, v, zf, q, v, zb)


def _oproj_kernel(h_ref, mod_ref, y_ref, w_ref, o_ref):
    o_ref[...] = h_ref[...] + mod_ref[5:6, :] * _dot(y_ref[...], w_ref[...])


def _oproj_split_kernel(h_ref, mod_ref, yl_ref, yc_ref, w_ref, o_ref, *, lat_tiles):
    i = pl.program_id(0)

    @pl.when(i < lat_tiles)
    def _():
        o_ref[...] = h_ref[...] + mod_ref[5:6, :] * _dot(yl_ref[...], w_ref[...])

    @pl.when(i >= lat_tiles)
    def _():
        o_ref[...] = h_ref[...] + mod_ref[5:6, :] * _dot(yc_ref[...], w_ref[...])


def _oproj_hgrn_kernel(h_ref, mod_ref, of_ref, ob_ref, g_ref, ng_ref, w_ref, o_ref):
    o = of_ref[...] + ob_ref[...]
    ng = ng_ref[...]
    ys = []
    for n in range(0, o.shape[1], HGRN_HEAD_DIM):
        sl = slice(n, n + HGRN_HEAD_DIM)
        ys.append((_rms(o[:, sl]) * ng * g_ref[:, sl].astype(F32)).astype(BF16))
    y = jnp.concatenate(ys, axis=1)
    o_ref[...] = h_ref[...] + mod_ref[5:6, :] * _dot(y, w_ref[...])


def _oproj_call(kernel, H, mod, layer, ys, small, w, j, n_tiles, plan, name, y_specs=None):
    NT, D = H.shape
    TM = plan.TM
    if y_specs is None:
        y_specs = [pl.BlockSpec((TM, y.shape[1]), lambda i: (i, 0)) for y in ys]
    in_specs = ([pl.BlockSpec((TM, D), lambda i: (i, 0)),
                 pl.BlockSpec((None, None, N_MOD, D), lambda i: (layer, plan.mod_row(i), 0, 0))]
                + y_specs
                + [pl.BlockSpec(s.shape, lambda i: (0, 0)) for s in small]
                + [_resident((None,) + w.shape[1:], lambda i: (j, 0, 0))])
    return pl.pallas_call(
        kernel,
        out_shape=jax.ShapeDtypeStruct((NT, D), F32),
        grid=(n_tiles,),
        in_specs=in_specs,
        out_specs=pl.BlockSpec((TM, D), lambda i: (i, 0)),
        input_output_aliases={0: 0},
        compiler_params=_params("arbitrary"),
        name=name,
    )(H, mod, *ys, *small, w)


def kernel(x, c, ctx, c_ctx, mod_w, mod_b, ffn1_w13, ffn1_w2, ffn2_w13, ffn2_w2, diff_w_in, diff_w_out, diff_lambda, diff_subln_g, ret_w_in, ret_w_out, ret_decay_exp, hgrn_w_in, hgrn_w_out, hgrn_lb_logits, hgrn_norm_g, gqa_w_in, gqa_w_out, gqa_q_norm_g, gqa_k_norm_g, final_norm_g):
    B, T, D = x.shape
    L = ctx.shape[1]
    depth = mod_w.shape[0]
    plan = _Plan(B, T, L)

    cstack = jnp.concatenate([c, c_ctx[None, :], jnp.zeros((MOD_ROWS - B - 1, D), F32)], axis=0)
    mod = _mod_call(cstack, mod_w, mod_b)
    H = (x.reshape(B * T, D), ctx.reshape(B * L, D))
    bf = lambda w: w.astype(BF16)
    ffn1_w13, ffn1_w2, ffn2_w13, ffn2_w2 = bf(ffn1_w13), bf(ffn1_w2), bf(ffn2_w13), bf(ffn2_w2)
    lat = plan.lat_tiles

    for i in range(depth):
        kind, j = i % N_MIXERS, i // N_MIXERS
        need_ctx = i < depth - 1
        mix_tiles = plan.all_tiles if need_ctx else plan.lat_tiles

        H = _ffn_call(H, mod, i, 0, ffn1_w13, ffn1_w2, plan.all_tiles, plan)

        if kind == 0:
            d = D // DIFF_HEADS // 2
            lam_init = 0.8 - 0.6 * math.exp(-0.3 * i)
            tabs = _rope_tables(plan, d, LANES)
            w_in = bf(diff_w_in)
            w_in = jnp.concatenate([_pair_halves(w_in[..., :D], d), _pair_halves(w_in[..., D:2 * D], d),
                                    w_in[..., 2 * D:]], axis=-1)
            q, k, v = _proj_call(
                functools.partial(_proj_diff_kernel, D=D, d=d), H, mod, i, w_in, j,
                tabs, _table_specs(plan, tabs), [(D, BF16)] * 3, plan, "proj_diff")
            small = [diff_lambda[j], diff_subln_g[j].reshape(1, 2 * d)]
            kern = functools.partial(_diff_attn_kernel, lam_init=lam_init, L=L, T=T)
            y = _attn_call(kern, plan, q, k, v, lambda h: h, DIFF_HEADS, LANES, small,
                           tq_lat=DIFF_TQ, name="diff_attn")
            if need_ctx:
                y = _attn_call(kern, plan, q, k, v, lambda h: h, DIFF_HEADS, LANES, small,
                               tq_lat=DIFF_TQ, ctx_out=y, name="diff_attn_ctx")
            H = _oproj_call(_oproj_kernel, H, mod, i, [y], [], bf(diff_w_out), j, mix_tiles, plan, "oproj")
        elif kind == 1:
            dk = D // RET_HEADS
            tabs = _rope_tables(plan, dk, dk)
            q, k, v, g = _proj_call(
                functools.partial(_proj_ret_kernel, D=D, dk=dk), H, mod, i, bf(ret_w_in), j,
                tabs, _table_specs(plan, tabs), [(D, BF16), (D, BF16), (2 * D, BF16), (2 * D, BF16)],
                plan, "proj_ret")
            yl, yc = _ret_call(plan, q, k, v, g, ret_decay_exp[j])
            y_specs = [pl.BlockSpec((plan.TM, 2 * D), lambda t: (jnp.minimum(t, lat - 1), 0)),
                       pl.BlockSpec((plan.TM, 2 * D), lambda t: (jnp.maximum(t - lat, 0), 0))]
            H = _oproj_call(functools.partial(_oproj_split_kernel, lat_tiles=lat), H, mod, i, [yl, yc], [],
                            bf(ret_w_out), j, mix_tiles, plan, "oproj_ret", y_specs=y_specs)
        elif kind == 2:
            q, v, g, zf, zb = _proj_call(
                functools.partial(_proj_hgrn_kernel, D=D), H, mod, i, bf(hgrn_w_in), j,
                [], [], [(D, BF16), (D, BF16), (D, BF16), (D, F32), (D, F32)], plan, "proj_hgrn")
            o_f, o_b = _hgrn_call(plan, i, hgrn_lb_logits, q, v, zf, zb)
            H = _oproj_call(_oproj_hgrn_kernel, H, mod, i, [o_f, o_b, g],
                            [hgrn_norm_g[j].reshape(1, HGRN_HEAD_DIM)], bf(hgrn_w_out), j,
                            mix_tiles, plan, "oproj_hgrn")
        else:
            d = GQA_HEAD_DIM
            kvw = GQA_KV_HEADS * d
            G = D // d // GQA_KV_HEADS
            tabs = _rope_tables(plan, d, LANES)
            extra = list(tabs) + [_pair_halves(gqa_q_norm_g[j].reshape(1, d), d),
                                  _pair_halves(gqa_k_norm_g[j].reshape(1, d), d)]
            specs = _table_specs(plan, tabs) + [pl.BlockSpec((1, d), lambda t: (0, 0))] * 2
            w_in = bf(gqa_w_in)
            w_in = jnp.concatenate([_pair_halves(w_in[..., :D], d), _pair_halves(w_in[..., D:D + kvw], d),
                                    w_in[..., D + kvw:]], axis=-1)
            q, k, v = _proj_call(
                functools.partial(_proj_gqa_kernel, D=D, d=d, kvw=kvw), H, mod, i, w_in, j,
                extra, specs, [(D, BF16), (kvw, BF16), (kvw, BF16)], plan, "proj_gqa")
            kern = functools.partial(_gqa_attn_kernel, G=G, L=L, T=T)
            y = _attn_call(kern, plan, q, k, v, lambda h: h, GQA_KV_HEADS, G * d, [],
                           tq_lat=GQA_TQ, name="gqa_attn")
            if need_ctx:
                y = _attn_call(kern, plan, q, k, v, lambda h: h, GQA_KV_HEADS, G * d, [],
                               tq_lat=GQA_TQ, ctx_out=y, name="gqa_attn_ctx")
            H = _oproj_call(_oproj_kernel, H, mod, i, [y], [], bf(gqa_w_out), j, mix_tiles, plan, "oproj")

        if need_ctx:
            H = _ffn_call(H, mod, i, 6, ffn2_w13, ffn2_w2, plan.all_tiles, plan)
        else:
            H = _ffn_call(H, mod, i, 6, ffn2_w13, ffn2_w2, plan.lat_tiles, plan, final_g=final_norm_g)
    return H.reshape(B, T, D)
```

```python
import functools
import math

import jax
import jax.numpy as jnp
import numpy as np
from jax import lax
from jax.experimental import pallas as pl
from jax.experimental.pallas import tpu as pltpu

F32 = jnp.float32
BF16 = jnp.bfloat16

NORM_EPS = 1e-6
ROPE_THETA = 10000.0
GRID_W = 64
N_MOD = 9
N_MIXERS = 4
LOG2E = math.log2(math.e)

LANES = 128
MOD_ROWS = 8
VMEM_LIMIT = 56 * 1024 * 1024

DIFF_HEADS = 8
RET_HEADS = 4
GQA_KV_HEADS = 2
GQA_HEAD_DIM = 128
HGRN_HEAD_DIM = 128
FFN_CHUNK = 256
RET_CHUNK = 256
RET_UNROLL = 4
HGRN_CHUNK = 128
HGRN_BLOCK = 256
ATTN_KV_CHUNK = 256
DIFF_TQ = 1024
GQA_TQ = 512


def _params(*sem):
    return pltpu.CompilerParams(dimension_semantics=sem, vmem_limit_bytes=VMEM_LIMIT)


def _dot(a, b):
    return jnp.dot(a, b, preferred_element_type=F32)


def _dot_nt(a, b):
    return lax.dot_general(a, b, (((1,), (1,)), ((), ())), preferred_element_type=F32)


def _dot_tn(a, b):
    return lax.dot_general(a, b, (((0,), (0,)), ((), ())), preferred_element_type=F32)


def _silu(x):
    return x * jax.nn.sigmoid(x)


def _rms(x):
    return x * lax.rsqrt(jnp.mean(x * x, axis=-1, keepdims=True) + NORM_EPS)


def _rms_block(y):
    sq = y * y
    hi = sq.astype(BF16)
    lo = (sq - hi.astype(F32)).astype(BF16)
    ones = jnp.ones((LANES, LANES), BF16)
    ms = (_dot(hi, ones) + _dot(lo, ones)) * (1.0 / LANES)
    return y * lax.rsqrt(ms + NORM_EPS)


def _modulated(h, mod_ref, k0):
    shift = mod_ref[k0:k0 + 1, :]
    scale = mod_ref[k0 + 1:k0 + 2, :]
    return _rms(h) * (1.0 + scale) + shift


HALF = LANES // 2


def _rope(y, c, s):
    return y * c + pltpu.roll(y, HALF, 1) * s


def _pair_halves(w, d):
    nq = d // 4
    lead = w.shape[:-1]
    x = w.reshape(*lead, -1, LANES // d, 2, 2, nq)
    return jnp.moveaxis(x, -2, -4).reshape(*lead, -1)


class _Plan:
    def __init__(self, B, T, L):
        self.B, self.T, self.L = B, T, L
        self.n_lat = B * T
        self.n_ctx = B * L
        self.NT = self.n_lat + self.n_ctx
        for tm in (1024, 512, 256):
            if T % tm == 0 and self.n_ctx % tm == 0:
                self.TM = tm
                break
        else:
            raise ValueError("unsupported sequence lengths")
        assert L == RET_CHUNK == HGRN_BLOCK and T % L == 0 and T % GRID_W == 0
        assert B + 1 <= MOD_ROWS
        self.tiles_per_batch = T // self.TM
        self.lat_tiles = self.n_lat // self.TM
        self.all_tiles = self.NT // self.TM

    def mod_row(self, i):
        return jnp.where(i < self.lat_tiles, i // self.tiles_per_batch, self.B)

    def pos_block(self, i):
        return jnp.where(i < self.lat_tiles, i % self.tiles_per_batch, self.tiles_per_batch)


def _rope_tables(plan, d, width):
    T = plan.T
    rows = T // GRID_W
    row = np.repeat(np.arange(rows, dtype=np.float32), GRID_W)
    col = np.tile(np.arange(GRID_W, dtype=np.float32), rows)
    nq = d // 4
    inv = (np.float32(ROPE_THETA) ** (-np.arange(nq, dtype=np.float32) * np.float32(2.0) / np.float32(d // 2)))
    inv = inv.astype(np.float32)
    ar, ac = row[:, None] * inv, col[:, None] * inv
    cr, sr, cc, sc = np.cos(ar), np.sin(ar), np.cos(ac), np.sin(ac)
    if 2 * nq == LANES:
        c = np.concatenate([cr, cr, cc, cc], axis=1)
        s = np.concatenate([-sr, sr, -sc, sc], axis=1)
    else:
        units = LANES // d
        c = np.concatenate([cr, cc] * (2 * units), axis=1)
        s = np.concatenate([-sr, -sc] * units + [sr, sc] * units, axis=1)
    assert c.shape[1] == width

    def finish(t, fill):
        t = np.concatenate([t, np.full((plan.TM, width), fill, np.float32)], axis=0)
        return jnp.asarray(t.astype(np.float32))

    return finish(c, 1.0), finish(s, 0.0)


def _mod_kernel(c_ref, w_ref, b_ref, o_ref):
    cond = _silu(c_ref[...]).astype(BF16)
    o_ref[...] = _dot(cond, w_ref[...].astype(BF16)) + b_ref[...]


def _mod_call(cstack, mod_w, mod_b):
    depth, D, _ = mod_w.shape
    out = pl.pallas_call(
        _mod_kernel,
        out_shape=jax.ShapeDtypeStruct((depth, MOD_ROWS, N_MOD * D), F32),
        grid=(depth, N_MOD),
        in_specs=[
            pl.BlockSpec((MOD_ROWS, D), lambda l, n: (0, 0)),
            pl.BlockSpec((None, D, D), lambda l, n: (l, 0, n)),
            pl.BlockSpec((None, 1, D), lambda l, n: (l, 0, n)),
        ],
        out_specs=pl.BlockSpec((None, MOD_ROWS, D), lambda l, n: (l, 0, n)),
        compiler_params=_params("arbitrary", "arbitrary"),
        name="mod_table",
    )(cstack, mod_w, mod_b.reshape(depth, 1, N_MOD * D))
    return out.reshape(depth, MOD_ROWS, N_MOD, D)


def _ffn_kernel(*refs, k0, F, lat_tiles, split_in, final):
    refs = list(refs)
    if split_in:
        hl_ref, hc_ref = refs[:2]
        x = jnp.where(pl.program_id(0) < lat_tiles, hl_ref[...], hc_ref[...])
        refs = refs[2:]
    else:
        x = refs[0][...]
        refs = refs[1:]
    mod_ref, w13_ref, w2_ref = refs[:3]
    o_ref = refs[-1]
    xn = _modulated(x, mod_ref, k0).astype(BF16)
    acc = None
    for lo in range(0, F, FFN_CHUNK):
        a = _dot(xn, w13_ref[:, lo:lo + FFN_CHUNK])
        b = _dot(xn, w13_ref[:, F + lo:F + lo + FFN_CHUNK])
        part = _dot((_silu(a) * b).astype(BF16), w2_ref[lo:lo + FFN_CHUNK, :])
        acc = part if acc is None else acc + part
    hn = x + 0.5 * mod_ref[k0 + 2:k0 + 3, :] * acc
    if final:
        hn = _rms(hn) * refs[3][...]
    o_ref[...] = hn


def _resident(shape, index_map):
    return pl.BlockSpec(shape, index_map, pipeline_mode=pl.Buffered(1))


def _ffn_call(h_in, mod, layer, k0, w13, w2, n_tiles, plan, final_g=None):
    split_in = isinstance(h_in, tuple)
    D, F = w2.shape[2], w2.shape[1]
    TM = plan.TM
    lat = plan.lat_tiles
    final = final_g is not None
    if split_in:
        in_specs = [pl.BlockSpec((TM, D), lambda i: (jnp.minimum(i, lat - 1), 0)),
                    pl.BlockSpec((TM, D), lambda i: (jnp.maximum(i - lat, 0), 0))]
        args = list(h_in)
    else:
        in_specs = [pl.BlockSpec((TM, D), lambda i: (i, 0))]
        args = [h_in]
    in_specs += [
        pl.BlockSpec((None, None, N_MOD, D), lambda i: (layer, plan.mod_row(i), 0, 0)),
        _resident((None, D, 2 * F), lambda i: (layer, 0, 0)),
        _resident((None, F, D), lambda i: (layer, 0, 0)),
    ]
    args += [mod, w13, w2]
    if final:
        in_specs.append(pl.BlockSpec((1, D), lambda i: (0, 0)))
        args.append(final_g.reshape(1, D))
    aliases = {} if (final or split_in) else {0: 0}
    rows = n_tiles * TM if final else plan.NT
    return pl.pallas_call(
        functools.partial(_ffn_kernel, k0=k0, F=F, lat_tiles=lat, split_in=split_in, final=final),
        out_shape=jax.ShapeDtypeStruct((rows, D), F32),
        grid=(n_tiles,),
        in_specs=in_specs,
        out_specs=pl.BlockSpec((TM, D), lambda i: (i, 0)),
        input_output_aliases=aliases,
        compiler_params=_params("arbitrary"),
        name="ffn_final" if final else "ffn",
    )(*args)


PROJ_COLS = 256


def _proj_diff_kernel(h_ref, mod_ref, w_ref, c_ref, s_ref, q_ref, k_ref, v_ref, *, D, d):
    xm = _modulated(h_ref[...], mod_ref, 3).astype(BF16)
    c, s = c_ref[...], s_ref[...]
    for n in range(0, D, PROJ_COLS):
        yq = _dot(xm, w_ref[:, n:n + PROJ_COLS]) * (d ** -0.5 * LOG2E)
        yk = _dot(xm, w_ref[:, D + n:D + n + PROJ_COLS])
        for m in range(0, PROJ_COLS, LANES):
            q_ref[:, n + m:n + m + LANES] = _rope(yq[:, m:m + LANES], c, s).astype(BF16)
            k_ref[:, n + m:n + m + LANES] = _rope(yk[:, m:m + LANES], c, s).astype(BF16)
        v_ref[:, n:n + PROJ_COLS] = _dot(xm, w_ref[:, 2 * D + n:2 * D + n + PROJ_COLS]).astype(BF16)


def _proj_ret_kernel(h_ref, mod_ref, w_ref, c_ref, s_ref, q_ref, k_ref, v_ref, g_ref, *, D, dk):
    xm = _modulated(h_ref[...], mod_ref, 3).astype(BF16)
    c, s = c_ref[...], s_ref[...]
    for n in range(0, D, PROJ_COLS):
        yq = _dot(xm, w_ref[:, n:n + PROJ_COLS])
        yk = _dot(xm, w_ref[:, D + n:D + n + PROJ_COLS]) * (dk ** -0.5)
        for m in range(0, PROJ_COLS, LANES):
            cm, sm = c[:, m:m + LANES], s[:, m:m + LANES]
            q_ref[:, n + m:n + m + LANES] = _rope(yq[:, m:m + LANES], cm, sm).astype(BF16)
            k_ref[:, n + m:n + m + LANES] = _rope(yk[:, m:m + LANES], cm, sm).astype(BF16)
    for n in range(0, 2 * D, PROJ_COLS):
        v_ref[:, n:n + PROJ_COLS] = _dot(xm, w_ref[:, 2 * D + n:2 * D + n + PROJ_COLS]).astype(BF16)
        g_ref[:, n:n + PROJ_COLS] = _silu(_dot(xm, w_ref[:, 4 * D + n:4 * D + n + PROJ_COLS])).astype(BF16)


def _proj_hgrn_kernel(h_ref, mod_ref, w_ref, q_ref, i_ref, g_ref, zf_ref, zb_ref, *, D):
    xm = _modulated(h_ref[...], mod_ref, 3).astype(BF16)
    for n in range(0, D, PROJ_COLS):
        sl = slice(n, n + PROJ_COLS)
        q_ref[:, sl] = _silu(_dot(xm, w_ref[:, n:n + PROJ_COLS])).astype(BF16)
        i_ref[:, sl] = _dot(xm, w_ref[:, D + n:D + n + PROJ_COLS]).astype(BF16)
        g_ref[:, sl] = _silu(_dot(xm, w_ref[:, 2 * D + n:2 * D + n + PROJ_COLS])).astype(BF16)
        zf_ref[:, sl] = _dot(xm, w_ref[:, 3 * D + n:3 * D + n + PROJ_COLS])
        zb_ref[:, sl] = _dot(xm, w_ref[:, 4 * D + n:4 * D + n + PROJ_COLS])


def _proj_gqa_kernel(h_ref, mod_ref, w_ref, c_ref, s_ref, qg_ref, kg_ref,
                     q_ref, k_ref, v_ref, *, D, d, kvw):
    xm = _modulated(h_ref[...], mod_ref, 3).astype(BF16)
    c, s = c_ref[...], s_ref[...]
    qg = qg_ref[...] * (d ** -0.5 * LOG2E)
    kg = kg_ref[...]
    for n in range(0, D, PROJ_COLS):
        yq = _dot(xm, w_ref[:, n:n + PROJ_COLS])
        for m in range(0, PROJ_COLS, LANES):
            q_ref[:, n + m:n + m + LANES] = _rope(_rms_block(yq[:, m:m + LANES]) * qg, c, s).astype(BF16)
    yk = _dot(xm, w_ref[:, D:D + kvw])
    for m in range(0, kvw, LANES):
        k_ref[:, m:m + LANES] = _rope(_rms_block(yk[:, m:m + LANES]) * kg, c, s).astype(BF16)
    v_ref[...] = _dot(xm, w_ref[:, D + kvw:D + 2 * kvw]).astype(BF16)


def _proj_call(kernel, H, mod, layer, w, j, extra, extra_specs, outs, plan, name):
    NT, D = H.shape
    TM = plan.TM
    in_specs = [
        pl.BlockSpec((TM, D), lambda i: (i, 0)),
        pl.BlockSpec((None, None, N_MOD, D), lambda i: (layer, plan.mod_row(i), 0, 0)),
        _resident((None,) + w.shape[1:], lambda i: (j, 0, 0)),
    ] + extra_specs
    return pl.pallas_call(
        kernel,
        out_shape=[jax.ShapeDtypeStruct((NT, wd), dt) for wd, dt in outs],
        grid=(plan.all_tiles,),
        in_specs=in_specs,
        out_specs=[pl.BlockSpec((TM, wd), lambda i: (i, 0)) for wd, _ in outs],
        compiler_params=_params("arbitrary"),
        name=name,
    )(H, mod, w, *extra)


def _table_specs(plan, tables):
    return [pl.BlockSpec((plan.TM, t.shape[1]), lambda i: (plan.pos_block(i), 0)) for t in tables]


def _flash(q, sources):
    m = l = acc = None
    for k_ref, v_ref, start, size in sources:
        k = k_ref[start:start + size, :]
        v = v_ref[start:start + size, :]
        s = _dot_nt(q, k)
        tiles = [s[:, t:t + LANES] for t in range(0, size, LANES)]
        mx = functools.reduce(jnp.maximum, tiles)
        ms = jnp.broadcast_to(jnp.max(mx, axis=1, keepdims=True), mx.shape)
        if m is None:
            m = ms
            ps = [jnp.exp2(t - m) for t in tiles]
            l = functools.reduce(jnp.add, ps)
            acc = _dot(jnp.concatenate(ps, axis=1).astype(BF16), v)
        else:
            m_new = jnp.maximum(m, ms)
            alpha = jnp.exp2(m - m_new)
            ps = [jnp.exp2(t - m_new) for t in tiles]
            l = alpha * l + functools.reduce(jnp.add, ps)
            acc = alpha * acc + _dot(jnp.concatenate(ps, axis=1).astype(BF16), v)
            m = m_new
    return acc, jnp.sum(l, axis=1, keepdims=True)


def _kv_sources(kv_refs, L, T):
    if len(kv_refs) == 2:
        kc, vc = kv_refs
        return [(kc, vc, 0, L)]
    kc, vc, kl, vl = kv_refs
    chunk = min(ATTN_KV_CHUNK, T)
    return [(kc, vc, 0, L)] + [(kl, vl, s, chunk) for s in range(0, T, chunk)]


def _diff_attn_kernel(*refs, lam_init, L, T):
    lam_ref, g_ref, q_ref = refs[:3]
    kv_refs, o_ref = refs[3:-1], refs[-1]
    q = q_ref[...]
    tq = q.shape[0]
    lane = lax.broadcasted_iota(jnp.int32, q.shape, 1)
    map0 = (lane & (HALF // 2)) == 0
    zero = jnp.zeros_like(q)
    qs = jnp.concatenate([jnp.where(map0, q, zero), jnp.where(map0, zero, q)], axis=0)
    acc, l = _flash(qs, _kv_sources(kv_refs, L, T))
    lam = lam_ref[...]
    lam_full = (jnp.exp(jnp.sum(lam[0:1] * lam[1:2], axis=1, keepdims=True))
                - jnp.exp(jnp.sum(lam[2:3] * lam[3:4], axis=1, keepdims=True)) + lam_init)
    o = acc[:tq] / l[:tq] - lam_full * (acc[tq:] / l[tq:])
    o_ref[...] = (_rms(o) * g_ref[...] * (1.0 - lam_init)).astype(BF16)


def _gqa_attn_kernel(*refs, G, L, T):
    q_ref = refs[0]
    kv_refs, o_ref = refs[1:-1], refs[-1]
    tq = q_ref.shape[0]
    qs = jnp.concatenate([q_ref[:, g * LANES:(g + 1) * LANES] for g in range(G)], axis=0)
    acc, l = _flash(qs, _kv_sources(kv_refs, L, T))
    o = acc / l
    for g in range(G):
        o_ref[:, g * LANES:(g + 1) * LANES] = o[g * tq:(g + 1) * tq].astype(BF16)


def _attn_call(kernel, plan, q, k, v, kv_col, n_heads, q_width, small, *, tq_lat, ctx_out=None, name):
    B, T, L = plan.B, plan.T, plan.L
    NT = q.shape[0]
    ctx_row = plan.n_lat // L
    kvw = LANES
    small_specs = [pl.BlockSpec(s.shape, lambda b, h, i: (0, 0)) for s in small]
    kv_specs = [pl.BlockSpec((L, kvw), lambda b, h, i: (ctx_row + b, kv_col(h)))] * 2
    args = list(small) + [q, k, v]
    if ctx_out is None:
        tq = min(tq_lat, T)
        nq = T // tq
        q_spec = pl.BlockSpec((tq, q_width), lambda b, h, i: (b * nq + i, h))
        kv_specs = kv_specs + [pl.BlockSpec((T, kvw), lambda b, h, i: (b, kv_col(h)))] * 2
        args += [k, v]
        aliases = {}
    else:
        tq, nq = L, 1
        q_spec = pl.BlockSpec((tq, q_width), lambda b, h, i: (ctx_row + b, h))
        args.append(ctx_out)
        kv_specs = kv_specs + [pl.BlockSpec(memory_space=pl.ANY)]
        aliases = {len(args) - 1: 0}
    if ctx_out is not None:
        body = lambda *r: kernel(*r[:-2], r[-1])
    else:
        body = kernel
    return pl.pallas_call(
        body,
        out_shape=jax.ShapeDtypeStruct((NT, q.shape[1]), BF16),
        grid=(B, n_heads, nq),
        in_specs=small_specs + [q_spec] + kv_specs,
        out_specs=q_spec,
        input_output_aliases=aliases,
        compiler_params=_params("arbitrary", "arbitrary", "arbitrary"),
        name=name,
    )(*args)


def _ret_kernel(dec_ref, qc_ref, kc_ref, vc_ref, gc_ref, ql_ref, kl_ref, vl_ref, gl_ref,
                yc_ref, yl_ref, sb_ref, st_ref, *, C, n_chunks):
    lg = jnp.log(1.0 - jnp.exp(-dec_ref[...] * math.log(2.0)))
    lgf, lgb = lg[0:1], lg[1:2]
    i = lax.broadcasted_iota(jnp.int32, (C, 1), 0).astype(F32)
    qdf, kef = jnp.exp((i + 1.0) * lgf), jnp.exp((C - 1.0 - i) * lgf)
    qdb, keb = jnp.exp((C - i) * lgb), jnp.exp(i * lgb)
    gfc, gbc = jnp.exp(C * lgf), jnp.exp(C * lgb)
    dist = (lax.broadcasted_iota(jnp.int32, (C, C), 0) - lax.broadcasted_iota(jnp.int32, (C, C), 1)).astype(F32)
    w = jnp.where(dist > 0, jnp.exp(jnp.maximum(dist, 0.0) * lgf),
                  jnp.where(dist < 0, jnp.exp(jnp.maximum(-dist, 0.0) * lgb), 2.0))

    def intra(q, k, v):
        return _dot((_dot_nt(q, k) * w).astype(BF16), v)

    def kv_state(k, v, ke):
        return _dot_tn((k.astype(F32) * ke).astype(BF16), v)

    def readout(o, g):
        return (g.astype(F32) * _rms(o)).astype(BF16)

    qx, kx, vx = qc_ref[...], kc_ref[...], vc_ref[...]
    yc_ref[...] = readout(intra(qx, kx, vx), gc_ref[...])

    st_ref[...] = kv_state(kx, vx, keb)

    def bwd(t, carry):
        c = n_chunks - 1 - t
        rows = pl.ds(pl.multiple_of(c * C, C), C)
        s = st_ref[...]
        sb_ref[c] = s.astype(BF16)
        st_ref[...] = gbc * s + kv_state(kl_ref[rows, :], vl_ref[rows, :], keb)
        return carry

    lax.fori_loop(0, n_chunks, bwd, 0, unroll=RET_UNROLL)

    st_ref[...] = kv_state(kx, vx, kef)

    def fwd(c, carry):
        rows = pl.ds(pl.multiple_of(c * C, C), C)
        q, k, v = ql_ref[rows, :], kl_ref[rows, :], vl_ref[rows, :]
        qf = q.astype(F32)
        s = st_ref[...]
        o = (intra(q, k, v) + _dot((qf * qdf).astype(BF16), s.astype(BF16))
             + _dot((qf * qdb).astype(BF16), sb_ref[c]))
        yl_ref[rows, :] = readout(o, gl_ref[rows, :])
        st_ref[...] = gfc * s + kv_state(k, v, kef)
        return carry

    lax.fori_loop(0, n_chunks, fwd, 0, unroll=RET_UNROLL)


def _ret_call(plan, q, k, v, g, decay_exp):
    B, T, L = plan.B, plan.T, plan.L
    NT, D = q.shape
    H = RET_HEADS
    dk, dv = D // H, v.shape[1] // H
    C = RET_CHUNK
    n_chunks = T // C
    ctx_row = plan.n_lat // L
    dec = jnp.transpose(decay_exp.astype(F32)).reshape(H, 2, 1)
    ctx = lambda w: pl.BlockSpec((L, w), lambda b, h: (ctx_row + b, h))
    lat = lambda w: pl.BlockSpec((T, w), lambda b, h: (b, h))
    yc, yl = pl.pallas_call(
        functools.partial(_ret_kernel, C=C, n_chunks=n_chunks),
        out_shape=[jax.ShapeDtypeStruct((plan.n_ctx, v.shape[1]), BF16),
                   jax.ShapeDtypeStruct((plan.n_lat, v.shape[1]), BF16)],
        grid=(B, H),
        in_specs=[pl.BlockSpec((None, 2, 1), lambda b, h: (h, 0, 0)),
                  ctx(dk), ctx(dk), ctx(dv), ctx(dv), lat(dk), lat(dk), lat(dv), lat(dv)],
        out_specs=[pl.BlockSpec((L, dv), lambda b, h: (b, h)),
                   pl.BlockSpec((T, dv), lambda b, h: (b, h))],
        scratch_shapes=[pltpu.VMEM((n_chunks, dk, dv), BF16), pltpu.VMEM((dk, dv), F32)],
        compiler_params=_params("arbitrary", "arbitrary"),
        name="retention",
    )(dec, q, k, v, g, q, k, v, g)
    return yl, yc


def _hgrn_lower_bound(logits, layer):
    e = jnp.exp(logits - jnp.max(logits, axis=0, keepdims=True))
    p = e / jnp.sum(e, axis=0, keepdims=True)
    return jnp.sum(p[0:layer + 1], axis=0, keepdims=True) - p[0:1]


def _hgrn_kernel(lb_ref, qf_ref, vf_ref, zf_ref, qb_ref, vb_ref, zb_ref, of_ref, ob_ref,
                 sf_ref, sb_ref, *, layer, C, n_heads):
    @pl.when(pl.program_id(1) == 0)
    def _():
        sf_ref[...] = jnp.zeros_like(sf_ref)
        sb_ref[...] = jnp.zeros_like(sb_ref)

    lb = _hgrn_lower_bound(lb_ref[...], layer)
    TB, D = qf_ref.shape
    n_sub = TB // C
    r_i = lax.broadcasted_iota(jnp.int32, (TB, TB), 0)
    c_i = lax.broadcasted_iota(jnp.int32, (TB, TB), 1)
    same_chunk = (r_i // C) == (c_i // C)
    m_r = lax.broadcasted_iota(jnp.int32, (C, C), 0)
    m_c = lax.broadcasted_iota(jnp.int32, (C, C), 1)

    def direction(q_ref, v_ref, z_ref, o_ref, st_ref, causal, mask, mid, last, order):
        f = lb + (1.0 - lb) * jax.nn.sigmoid(z_ref[...])
        kk = 1.0 - f
        la = jnp.log(f)
        tri = jnp.where(same_chunk & causal, 1.0, 0.0).astype(BF16)
        hi = la.astype(BF16)
        r1 = la - hi.astype(F32)
        md = r1.astype(BF16)
        lo = (r1 - md.astype(F32)).astype(BF16)
        b = _dot(tri, hi) + _dot(tri, md) + _dot(tri, lo)
        refs = [b[j * C + mid:j * C + mid + 1] for j in range(n_sub)]
        tots = [b[j * C + last:j * C + last + 1] for j in range(n_sub)]
        ref_rows = jnp.concatenate([jnp.broadcast_to(r, (C, D)) for r in refs], axis=0)
        dl = (b - ref_rows) * LOG2E
        qd = q_ref[...].astype(F32) * jnp.exp2(dl)
        kd = kk * jnp.exp2(-dl)
        v = v_ref[...]
        for j in order:
            rows = slice(j * C, (j + 1) * C)
            qd_j, kd_j = qd[rows], kd[rows]
            qdb, kdb = qd_j.astype(BF16), kd_j.astype(BF16)
            qs = (qd_j * jnp.exp(refs[j])).astype(BF16)
            ke = (kd_j * jnp.exp(tots[j] - refs[j])).astype(BF16)
            dec = jnp.exp(tots[j])
            for h in range(n_heads):
                sl = slice(h * HGRN_HEAD_DIM, (h + 1) * HGRN_HEAD_DIM)
                att = jnp.where(mask, _dot_nt(qdb[:, sl], kdb[:, sl]), 0.0).astype(BF16)
                st = st_ref[h]
                o_ref[rows, sl] = _dot(att, v[rows, sl]) + _dot_nt(qs[:, sl], st.astype(BF16))
                st_ref[h] = st * dec[:, sl] + _dot_tn(v[rows, sl], ke[:, sl])

    direction(qf_ref, vf_ref, zf_ref, of_ref, sf_ref, r_i >= c_i, m_r >= m_c, C // 2 - 1, C - 1,
              range(n_sub))
    direction(qb_ref, vb_ref, zb_ref, ob_ref, sb_ref, r_i <= c_i, m_r <= m_c, C // 2, 0,
              range(n_sub - 1, -1, -1))


def _hgrn_call(plan, layer, lb_logits, q, v, zf, zb):
    B, T, L = plan.B, plan.T, plan.L
    NT, D = q.shape
    TB = HGRN_BLOCK
    nb = T // TB
    ctx_blk = plan.n_lat // TB
    H = D // HGRN_HEAD_DIM
    fwd = lambda b, s: (jnp.where(s == 0, ctx_blk + b, b * nb + s - 1), 0)
    bwd = lambda b, s: (jnp.where(s == 0, ctx_blk + b, b * nb + nb - s), 0)
    blk = lambda im: pl.BlockSpec((TB, D), im)
    return pl.pallas_call(
        functools.partial(_hgrn_kernel, layer=layer, C=HGRN_CHUNK, n_heads=H),
        out_shape=[jax.ShapeDtypeStruct((NT, D), F32), jax.ShapeDtypeStruct((NT, D), F32)],
        grid=(B, nb + 1),
        in_specs=[pl.BlockSpec(lb_logits.shape, lambda b, s: (0, 0)),
                  blk(fwd), blk(fwd), blk(fwd), blk(bwd), blk(bwd), blk(bwd)],
        out_specs=[blk(fwd), blk(bwd)],
        scratch_shapes=[pltpu.VMEM((H, HGRN_HEAD_DIM, HGRN_HEAD_DIM), F32),
                        pltpu.VMEM((H, HGRN_HEAD_DIM, HGRN_HEAD_DIM), F32)],
        compiler_params=_params("arbitrary", "arbitrary"),
        name="hgrn2",
    )(lb_logits, q, v, zf, q, v, zb)


def _oproj_kernel(h_ref, mod_ref, y_ref, w_ref, o_ref):
    o_ref[...] = h_ref[...] + mod_ref[5:6, :] * _dot(y_ref[...], w_ref[...])


def _oproj_split_kernel(h_ref, mod_ref, yl_ref, yc_ref, w_ref, o_ref, *, lat_tiles):
    i = pl.program_id(0)

    @pl.when(i < lat_tiles)
    def _():
        o_ref[...] = h_ref[...] + mod_ref[5:6, :] * _dot(yl_ref[...], w_ref[...])

    @pl.when(i >= lat_tiles)
    def _():
        o_ref[...] = h_ref[...] + mod_ref[5:6, :] * _dot(yc_ref[...], w_ref[...])


def _oproj_hgrn_kernel(h_ref, mod_ref, of_ref, ob_ref, g_ref, ng_ref, w_ref, o_ref):
    o = of_ref[...] + ob_ref[...]
    ng = ng_ref[...]
    ys = []
    for n in range(0, o.shape[1], HGRN_HEAD_DIM):
        sl = slice(n, n + HGRN_HEAD_DIM)
        ys.append((_rms(o[:, sl]) * ng * g_ref[:, sl].astype(F32)).astype(BF16))
    y = jnp.concatenate(ys, axis=1)
    o_ref[...] = h_ref[...] + mod_ref[5:6, :] * _dot(y, w_ref[...])


def _oproj_call(kernel, H, mod, layer, ys, small, w, j, n_tiles, plan, name, y_specs=None):
    NT, D = H.shape
    TM = plan.TM
    if y_specs is None:
        y_specs = [pl.BlockSpec((TM, y.shape[1]), lambda i: (i, 0)) for y in ys]
    in_specs = ([pl.BlockSpec((TM, D), lambda i: (i, 0)),
                 pl.BlockSpec((None, None, N_MOD, D), lambda i: (layer, plan.mod_row(i), 0, 0))]
                + y_specs
                + [pl.BlockSpec(s.shape, lambda i: (0, 0)) for s in small]
                + [_resident((None,) + w.shape[1:], lambda i: (j, 0, 0))])
    return pl.pallas_call(
        kernel,
        out_shape=jax.ShapeDtypeStruct((NT, D), F32),
        grid=(n_tiles,),
        in_specs=in_specs,
        out_specs=pl.BlockSpec((TM, D), lambda i: (i, 0)),
        input_output_aliases={0: 0},
        compiler_params=_params("arbitrary"),
        name=name,
    )(H, mod, *ys, *small, w)


def kernel(x, c, ctx, c_ctx, mod_w, mod_b, ffn1_w13, ffn1_w2, ffn2_w13, ffn2_w2, diff_w_in, diff_w_out, diff_lambda, diff_subln_g, ret_w_in, ret_w_out, ret_decay_exp, hgrn_w_in, hgrn_w_out, hgrn_lb_logits, hgrn_norm_g, gqa_w_in, gqa_w_out, gqa_q_norm_g, gqa_k_norm_g, final_norm_g):
    B, T, D = x.shape
    L = ctx.shape[1]
    depth = mod_w.shape[0]
    plan = _Plan(B, T, L)

    cstack = jnp.concatenate([c, c_ctx[None, :], jnp.zeros((MOD_ROWS - B - 1, D), F32)], axis=0)
    mod = _mod_call(cstack, mod_w, mod_b)
    H = (x.reshape(B * T, D), ctx.reshape(B * L, D))
    bf = lambda w: w.astype(BF16)
    ffn1_w13, ffn1_w2, ffn2_w13, ffn2_w2 = bf(ffn1_w13), bf(ffn1_w2), bf(ffn2_w13), bf(ffn2_w2)
    lat = plan.lat_tiles

    for i in range(depth):
        kind, j = i % N_MIXERS, i // N_MIXERS
        need_ctx = i < depth - 1
        mix_tiles = plan.all_tiles if need_ctx else plan.lat_tiles

        H = _ffn_call(H, mod, i, 0, ffn1_w13, ffn1_w2, plan.all_tiles, plan)

        if kind == 0:
            d = D // DIFF_HEADS // 2
            lam_init = 0.8 - 0.6 * math.exp(-0.3 * i)
            tabs = _rope_tables(plan, d, LANES)
            w_in = bf(diff_w_in)
            w_in = jnp.concatenate([_pair_halves(w_in[..., :D], d), _pair_halves(w_in[..., D:2 * D], d),
                                    w_in[..., 2 * D:]], axis=-1)
            q, k, v = _proj_call(
                functools.partial(_proj_diff_kernel, D=D, d=d), H, mod, i, w_in, j,
                tabs, _table_specs(plan, tabs), [(D, BF16)] * 3, plan, "proj_diff")
            small = [diff_lambda[j], diff_subln_g[j].reshape(1, 2 * d)]
            kern = functools.partial(_diff_attn_kernel, lam_init=lam_init, L=L, T=T)
            y = _attn_call(kern, plan, q, k, v, lambda h: h, DIFF_HEADS, LANES, small,
                           tq_lat=DIFF_TQ, name="diff_attn")
            if need_ctx:
                y = _attn_call(kern, plan, q, k, v, lambda h: h, DIFF_HEADS, LANES, small,
                               tq_lat=DIFF_TQ, ctx_out=y, name="diff_attn_ctx")
            H = _oproj_call(_oproj_kernel, H, mod, i, [y], [], bf(diff_w_out), j, mix_tiles, plan, "oproj")
        elif kind == 1:
            dk = D // RET_HEADS
            tabs = _rope_tables(plan, dk, dk)
            q, k, v, g = _proj_call(
                functools.partial(_proj_ret_kernel, D=D, dk=dk), H, mod, i, bf(ret_w_in), j,
                tabs, _table_specs(plan, tabs), [(D, BF16), (D, BF16), (2 * D, BF16), (2 * D, BF16)],
                plan, "proj_ret")
            yl, yc = _ret_call(plan, q, k, v, g, ret_decay_exp[j])
            y_specs = [pl.BlockSpec((plan.TM, 2 * D), lambda t: (jnp.minimum(t, lat - 1), 0)),
                       pl.BlockSpec((plan.TM, 2 * D), lambda t: (jnp.maximum(t - lat, 0), 0))]
            H = _oproj_call(functools.partial(_oproj_split_kernel, lat_tiles=lat), H, mod, i, [yl, yc], [],
                            bf(ret_w_out), j, mix_tiles, plan, "oproj_ret", y_specs=y_specs)
        elif kind == 2:
            q, v, g, zf, zb = _proj_call(
                functools.partial(_proj_hgrn_kernel, D=D), H, mod, i, bf(hgrn_w_in), j,
                [], [], [(D, BF16), (D, BF16), (D, BF16), (D, F32), (D, F32)], plan, "proj_hgrn")
            o_f, o_b = _hgrn_call(plan, i, hgrn_lb_logits, q, v, zf, zb)
            H = _oproj_call(_oproj_hgrn_kernel, H, mod, i, [o_f, o_b, g],
                            [hgrn_norm_g[j].reshape(1, HGRN_HEAD_DIM)], bf(hgrn_w_out), j,
                            mix_tiles, plan, "oproj_hgrn")
        else:
            d = GQA_HEAD_DIM
            kvw = GQA_KV_HEADS * d
            G = D // d // GQA_KV_HEADS
            tabs = _rope_tables(plan, d, LANES)
            extra = list(tabs) + [_pair_halves(gqa_q_norm_g[j].reshape(1, d), d),
                                  _pair_halves(gqa_k_norm_g[j].reshape(1, d), d)]
            specs = _table_specs(plan, tabs) + [pl.BlockSpec((1, d), lambda t: (0, 0))] * 2
            w_in = bf(gqa_w_in)
            w_in = jnp.concatenate([_pair_halves(w_in[..., :D], d), _pair_halves(w_in[..., D:D + kvw], d),
                                    w_in[..., D + kvw:]], axis=-1)
            q, k, v = _proj_call(
                functools.partial(_proj_gqa_kernel, D=D, d=d, kvw=kvw), H, mod, i, w_in, j,
                extra, specs, [(D, BF16), (kvw, BF16), (kvw, BF16)], plan, "proj_gqa")
            kern = functools.partial(_gqa_attn_kernel, G=G, L=L, T=T)
            y = _attn_call(kern, plan, q, k, v, lambda h: h, GQA_KV_HEADS, G * d, [],
                           tq_lat=GQA_TQ, name="gqa_attn")
            if need_ctx:
                y = _attn_call(kern, plan, q, k, v, lambda h: h, GQA_KV_HEADS, G * d, [],
                               tq_lat=GQA_TQ, ctx_out=y, name="gqa_attn_ctx")
            H = _oproj_call(_oproj_kernel, H, mod, i, [y], [], bf(gqa_w_out), j, mix_tiles, plan, "oproj")

        if need_ctx:
            H = _ffn_call(H, mod, i, 6, ffn2_w13, ffn2_w2, plan.all_tiles, plan)
        else:
            H = _ffn_call(H, mod, i, 6, ffn2_w13, ffn2_w2, plan.lat_tiles, plan, final_g=final_norm_g)
    return H.reshape(B, T, D)
```

```python
import functools
import math

import jax
import jax.numpy as jnp
import numpy as np
from jax import lax
from jax.experimental import pallas as pl
from jax.experimental.pallas import tpu as pltpu

F32 = jnp.float32
BF16 = jnp.bfloat16

NORM_EPS = 1e-6
ROPE_THETA = 10000.0
GRID_W = 64
N_MOD = 9
N_MIXERS = 4
LOG2E = math.log2(math.e)

LANES = 128
MOD_ROWS = 8
VMEM_LIMIT = 56 * 1024 * 1024

DIFF_HEADS = 8
RET_HEADS = 4
GQA_KV_HEADS = 2
GQA_HEAD_DIM = 128
HGRN_HEAD_DIM = 128
FFN_CHUNK = 256
RET_CHUNK = 256
RET_UNROLL = 4
HGRN_CHUNK = 128
HGRN_BLOCK = 256
ATTN_KV_CHUNK = 256
DIFF_TQ = 1024
GQA_TQ = 512


def _params(*sem):
    return pltpu.CompilerParams(dimension_semantics=sem, vmem_limit_bytes=VMEM_LIMIT)


def _dot(a, b):
    return jnp.dot(a, b, preferred_element_type=F32)


def _dot_nt(a, b):
    return lax.dot_general(a, b, (((1,), (1,)), ((), ())), preferred_element_type=F32)


def _dot_tn(a, b):
    return lax.dot_general(a, b, (((0,), (0,)), ((), ())), preferred_element_type=F32)


def _silu(x):
    return x * jax.nn.sigmoid(x)


def _rms(x):
    return x * lax.rsqrt(jnp.mean(x * x, axis=-1, keepdims=True) + NORM_EPS)


def _rms_block(y):
    sq = y * y
    hi = sq.astype(BF16)
    lo = (sq - hi.astype(F32)).astype(BF16)
    ones = jnp.ones((LANES, LANES), BF16)
    ms = (_dot(hi, ones) + _dot(lo, ones)) * (1.0 / LANES)
    return y * lax.rsqrt(ms + NORM_EPS)


def _modulated(h, mod_ref, k0):
    shift = mod_ref[k0:k0 + 1, :]
    scale = mod_ref[k0 + 1:k0 + 2, :]
    return _rms(h) * (1.0 + scale) + shift


HALF = LANES // 2


def _rope(y, c, s):
    return y * c + pltpu.roll(y, HALF, 1) * s


def _pair_halves(w, d):
    nq = d // 4
    lead = w.shape[:-1]
    x = w.reshape(*lead, -1, LANES // d, 2, 2, nq)
    return jnp.moveaxis(x, -2, -4).reshape(*lead, -1)


class _Plan:
    def __init__(self, B, T, L):
        self.B, self.T, self.L = B, T, L
        self.n_lat = B * T
        self.n_ctx = B * L
        self.NT = self.n_lat + self.n_ctx
        for tm in (1024, 512, 256):
            if T % tm == 0 and self.n_ctx % tm == 0:
                self.TM = tm
                break
        else:
            raise ValueError("unsupported sequence lengths")
        assert L == RET_CHUNK == HGRN_BLOCK and T % L == 0 and T % GRID_W == 0
        assert B + 1 <= MOD_ROWS
        self.tiles_per_batch = T // self.TM
        self.lat_tiles = self.n_lat // self.TM
        self.all_tiles = self.NT // self.TM

    def mod_row(self, i):
        return jnp.where(i < self.lat_tiles, i // self.tiles_per_batch, self.B)

    def pos_block(self, i):
        return jnp.where(i < self.lat_tiles, i % self.tiles_per_batch, self.tiles_per_batch)


def _rope_tables(plan, d, width):
    T = plan.T
    rows = T // GRID_W
    row = np.repeat(np.arange(rows, dtype=np.float32), GRID_W)
    col = np.tile(np.arange(GRID_W, dtype=np.float32), rows)
    nq = d // 4
    inv = (np.float32(ROPE_THETA) ** (-np.arange(nq, dtype=np.float32) * np.float32(2.0) / np.float32(d // 2)))
    inv = inv.astype(np.float32)
    ar, ac = row[:, None] * inv, col[:, None] * inv
    cr, sr, cc, sc = np.cos(ar), np.sin(ar), np.cos(ac), np.sin(ac)
    if 2 * nq == LANES:
        c = np.concatenate([cr, cr, cc, cc], axis=1)
        s = np.concatenate([-sr, sr, -sc, sc], axis=1)
    else:
        units = LANES // d
        c = np.concatenate([cr, cc] * (2 * units), axis=1)
        s = np.concatenate([-sr, -sc] * units + [sr, sc] * units, axis=1)
    assert c.shape[1] == width

    def finish(t, fill):
        t = np.concatenate([t, np.full((plan.TM, width), fill, np.float32)], axis=0)
        return jnp.asarray(t.astype(np.float32))

    return finish(c, 1.0), finish(s, 0.0)


def _mod_kernel(c_ref, w_ref, b_ref, o_ref):
    cond = _silu(c_ref[...]).astype(BF16)
    o_ref[...] = _dot(cond, w_ref[...].astype(BF16)) + b_ref[...]


def _mod_call(cstack, mod_w, mod_b):
    depth, D, _ = mod_w.shape
    out = pl.pallas_call(
        _mod_kernel,
        out_shape=jax.ShapeDtypeStruct((depth, MOD_ROWS, N_MOD * D), F32),
        grid=(depth, N_MOD),
        in_specs=[
            pl.BlockSpec((MOD_ROWS, D), lambda l, n: (0, 0)),
            pl.BlockSpec((None, D, D), lambda l, n: (l, 0, n)),
            pl.BlockSpec((None, 1, D), lambda l, n: (l, 0, n)),
        ],
        out_specs=pl.BlockSpec((None, MOD_ROWS, D), lambda l, n: (l, 0, n)),
        compiler_params=_params("arbitrary", "arbitrary"),
        name="mod_table",
    )(cstack, mod_w, mod_b.reshape(depth, 1, N_MOD * D))
    return out.reshape(depth, MOD_ROWS, N_MOD, D)


def _ffn_kernel(*refs, k0, F, lat_tiles, split_in, final, cast_next):
    refs = list(refs)
    if cast_next:
        c13_in, c2_in, o_ref, c13_out, c2_out = refs[-5:]
        c13_out[...] = c13_in[...].astype(BF16)
        c2_out[...] = c2_in[...].astype(BF16)
        refs = refs[:-5] + [o_ref]
    if split_in:
        hl_ref, hc_ref = refs[:2]
        x = jnp.where(pl.program_id(0) < lat_tiles, hl_ref[...], hc_ref[...])
        refs = refs[2:]
    else:
        x = refs[0][...]
        refs = refs[1:]
    mod_ref, w13_ref, w2_ref = refs[:3]
    o_ref = refs[-1]
    xn = _modulated(x, mod_ref, k0).astype(BF16)
    acc = None
    for lo in range(0, F, FFN_CHUNK):
        a = _dot(xn, w13_ref[:, lo:lo + FFN_CHUNK])
        b = _dot(xn, w13_ref[:, F + lo:F + lo + FFN_CHUNK])
        part = _dot((_silu(a) * b).astype(BF16), w2_ref[lo:lo + FFN_CHUNK, :])
        acc = part if acc is None else acc + part
    hn = x + 0.5 * mod_ref[k0 + 2:k0 + 3, :] * acc
    if final:
        hn = _rms(hn) * refs[3][...]
    o_ref[...] = hn


def _resident(shape, index_map):
    return pl.BlockSpec(shape, index_map, pipeline_mode=pl.Buffered(1))


BF16_ROWS = 16


def _cast_steps(n_tiles, *row_counts):
    for c in range(n_tiles, 0, -1):
        if all(r % (c * BF16_ROWS) == 0 for r in row_counts):
            return c
    raise ValueError("weights cannot be split into aligned slabs")


def _ffn_call(h_in, mod, layer, k0, w13, w2, widx, n_tiles, plan, final_g=None, cast_next=None):
    split_in = isinstance(h_in, tuple)
    D, F = w2.shape[2], w2.shape[1]
    TM = plan.TM
    lat = plan.lat_tiles
    final = final_g is not None
    if split_in:
        in_specs = [pl.BlockSpec((TM, D), lambda i: (jnp.minimum(i, lat - 1), 0)),
                    pl.BlockSpec((TM, D), lambda i: (jnp.maximum(i - lat, 0), 0))]
        args = list(h_in)
    else:
        in_specs = [pl.BlockSpec((TM, D), lambda i: (i, 0))]
        args = [h_in]
    in_specs += [
        pl.BlockSpec((None, None, N_MOD, D), lambda i: (layer, plan.mod_row(i), 0, 0)),
        _resident((None, D, 2 * F), lambda i: (widx, 0, 0)),
        _resident((None, F, D), lambda i: (widx, 0, 0)),
    ]
    args += [mod, w13, w2]
    if final:
        in_specs.append(pl.BlockSpec((1, D), lambda i: (0, 0)))
        args.append(final_g.reshape(1, D))
    aliases = {} if (final or split_in) else {0: 0}
    rows = n_tiles * TM if final else plan.NT
    out_shape = [jax.ShapeDtypeStruct((rows, D), F32)]
    out_specs = [pl.BlockSpec((TM, D), lambda i: (i, 0))]
    if cast_next is not None:
        n13, n2, nidx = cast_next
        steps = _cast_steps(n_tiles, D, F)
        slab = lambda i: (nidx, jnp.minimum(i, steps - 1), 0)
        out_slab = lambda i: (0, jnp.minimum(i, steps - 1), 0)
        in_specs += [pl.BlockSpec((None, D // steps, 2 * F), slab), pl.BlockSpec((None, F // steps, D), slab)]
        args += [n13, n2]
        out_shape += [jax.ShapeDtypeStruct((1, D, 2 * F), BF16), jax.ShapeDtypeStruct((1, F, D), BF16)]
        out_specs += [pl.BlockSpec((None, D // steps, 2 * F), out_slab),
                      pl.BlockSpec((None, F // steps, D), out_slab)]
    outs = pl.pallas_call(
        functools.partial(_ffn_kernel, k0=k0, F=F, lat_tiles=lat, split_in=split_in, final=final,
                          cast_next=cast_next is not None),
        out_shape=out_shape,
        grid=(n_tiles,),
        in_specs=in_specs,
        out_specs=out_specs,
        input_output_aliases=aliases,
        compiler_params=_params("arbitrary"),
        name="ffn_final" if final else "ffn",
    )(*args)
    return outs[0] if cast_next is None else outs


PROJ_COLS = 256


def _proj_diff_kernel(h_ref, mod_ref, w_ref, c_ref, s_ref, q_ref, k_ref, v_ref, *, D, d):
    xm = _modulated(h_ref[...], mod_ref, 3).astype(BF16)
    c, s = c_ref[...], s_ref[...]
    for n in range(0, D, PROJ_COLS):
        yq = _dot(xm, w_ref[:, n:n + PROJ_COLS]) * (d ** -0.5 * LOG2E)
        yk = _dot(xm, w_ref[:, D + n:D + n + PROJ_COLS])
        for m in range(0, PROJ_COLS, LANES):
            q_ref[:, n + m:n + m + LANES] = _rope(yq[:, m:m + LANES], c, s).astype(BF16)
            k_ref[:, n + m:n + m + LANES] = _rope(yk[:, m:m + LANES], c, s).astype(BF16)
        v_ref[:, n:n + PROJ_COLS] = _dot(xm, w_ref[:, 2 * D + n:2 * D + n + PROJ_COLS]).astype(BF16)


def _proj_ret_kernel(h_ref, mod_ref, w_ref, c_ref, s_ref, q_ref, k_ref, v_ref, g_ref, *, D, dk):
    xm = _modulated(h_ref[...], mod_ref, 3).astype(BF16)
    c, s = c_ref[...], s_ref[...]
    for n in range(0, D, PROJ_COLS):
        yq = _dot(xm, w_ref[:, n:n + PROJ_COLS])
        yk = _dot(xm, w_ref[:, D + n:D + n + PROJ_COLS]) * (dk ** -0.5)
        for m in range(0, PROJ_COLS, LANES):
            cm, sm = c[:, m:m + LANES], s[:, m:m + LANES]
            q_ref[:, n + m:n + m + LANES] = _rope(yq[:, m:m + LANES], cm, sm).astype(BF16)
            k_ref[:, n + m:n + m + LANES] = _rope(yk[:, m:m + LANES], cm, sm).astype(BF16)
    for n in range(0, 2 * D, PROJ_COLS):
        v_ref[:, n:n + PROJ_COLS] = _dot(xm, w_ref[:, 2 * D + n:2 * D + n + PROJ_COLS]).astype(BF16)
        g_ref[:, n:n + PROJ_COLS] = _silu(_dot(xm, w_ref[:, 4 * D + n:4 * D + n + PROJ_COLS])).astype(BF16)


def _proj_hgrn_kernel(h_ref, mod_ref, w_ref, q_ref, i_ref, g_ref, zf_ref, zb_ref, *, D):
    xm = _modulated(h_ref[...], mod_ref, 3).astype(BF16)
    for n in range(0, D, PROJ_COLS):
        sl = slice(n, n + PROJ_COLS)
        q_ref[:, sl] = _silu(_dot(xm, w_ref[:, n:n + PROJ_COLS])).astype(BF16)
        i_ref[:, sl] = _dot(xm, w_ref[:, D + n:D + n + PROJ_COLS]).astype(BF16)
        g_ref[:, sl] = _silu(_dot(xm, w_ref[:, 2 * D + n:2 * D + n + PROJ_COLS])).astype(BF16)
        zf_ref[:, sl] = _dot(xm, w_ref[:, 3 * D + n:3 * D + n + PROJ_COLS])
        zb_ref[:, sl] = _dot(xm, w_ref[:, 4 * D + n:4 * D + n + PROJ_COLS])


def _proj_gqa_kernel(h_ref, mod_ref, w_ref, c_ref, s_ref, qg_ref, kg_ref,
                     q_ref, k_ref, v_ref, *, D, d, kvw):
    xm = _modulated(h_ref[...], mod_ref, 3).astype(BF16)
    c, s = c_ref[...], s_ref[...]
    qg = qg_ref[...] * (d ** -0.5 * LOG2E)
    kg = kg_ref[...]
    for n in range(0, D, PROJ_COLS):
        yq = _dot(xm, w_ref[:, n:n + PROJ_COLS])
        for m in range(0, PROJ_COLS, LANES):
            q_ref[:, n + m:n + m + LANES] = _rope(_rms_block(yq[:, m:m + LANES]) * qg, c, s).astype(BF16)
    yk = _dot(xm, w_ref[:, D:D + kvw])
    for m in range(0, kvw, LANES):
        k_ref[:, m:m + LANES] = _rope(_rms_block(yk[:, m:m + LANES]) * kg, c, s).astype(BF16)
    v_ref[...] = _dot(xm, w_ref[:, D + kvw:D + 2 * kvw]).astype(BF16)


def _proj_call(kernel, H, mod, layer, w, j, extra, extra_specs, outs, plan, name):
    NT, D = H.shape
    TM = plan.TM
    in_specs = [
        pl.BlockSpec((TM, D), lambda i: (i, 0)),
        pl.BlockSpec((None, None, N_MOD, D), lambda i: (layer, plan.mod_row(i), 0, 0)),
        _resident((None,) + w.shape[1:], lambda i: (j, 0, 0)),
    ] + extra_specs
    return pl.pallas_call(
        kernel,
        out_shape=[jax.ShapeDtypeStruct((NT, wd), dt) for wd, dt in outs],
        grid=(plan.all_tiles,),
        in_specs=in_specs,
        out_specs=[pl.BlockSpec((TM, wd), lambda i: (i, 0)) for wd, _ in outs],
        compiler_params=_params("arbitrary"),
        name=name,
    )(H, mod, w, *extra)


def _table_specs(plan, tables):
    return [pl.BlockSpec((plan.TM, t.shape[1]), lambda i: (plan.pos_block(i), 0)) for t in tables]


def _flash(q, sources):
    m = l = acc = None
    for k_ref, v_ref, start, size in sources:
        k = k_ref[start:start + size, :]
        v = v_ref[start:start + size, :]
        s = _dot_nt(q, k)
        tiles = [s[:, t:t + LANES] for t in range(0, size, LANES)]
        mx = functools.reduce(jnp.maximum, tiles)
        ms = jnp.broadcast_to(jnp.max(mx, axis=1, keepdims=True), mx.shape)
        if m is None:
            m = ms
            ps = [jnp.exp2(t - m) for t in tiles]
            l = functools.reduce(jnp.add, ps)
            acc = _dot(jnp.concatenate(ps, axis=1).astype(BF16), v)
        else:
            m_new = jnp.maximum(m, ms)
            alpha = jnp.exp2(m - m_new)
            ps = [jnp.exp2(t - m_new) for t in tiles]
            l = alpha * l + functools.reduce(jnp.add, ps)
            acc = alpha * acc + _dot(jnp.concatenate(ps, axis=1).astype(BF16), v)
            m = m_new
    return acc, jnp.sum(l, axis=1, keepdims=True)


def _kv_sources(kv_refs, L, T):
    if len(kv_refs) == 2:
        kc, vc = kv_refs
        return [(kc, vc, 0, L)]
    kc, vc, kl, vl = kv_refs
    chunk = min(ATTN_KV_CHUNK, T)
    return [(kc, vc, 0, L)] + [(kl, vl, s, chunk) for s in range(0, T, chunk)]


def _diff_attn_kernel(*refs, lam_init, L, T):
    lam_ref, g_ref, q_ref = refs[:3]
    kv_refs, o_ref = refs[3:-1], refs[-1]
    q = q_ref[...]
    tq = q.shape[0]
    lane = lax.broadcasted_iota(jnp.int32, q.shape, 1)
    map0 = (lane & (HALF // 2)) == 0
    zero = jnp.zeros_like(q)
    qs = jnp.concatenate([jnp.where(map0, q, zero), jnp.where(map0, zero, q)], axis=0)
    acc, l = _flash(qs, _kv_sources(kv_refs, L, T))
    lam = lam_ref[...]
    lam_full = (jnp.exp(jnp.sum(lam[0:1] * lam[1:2], axis=1, keepdims=True))
                - jnp.exp(jnp.sum(lam[2:3] * lam[3:4], axis=1, keepdims=True)) + lam_init)
    o = acc[:tq] / l[:tq] - lam_full * (acc[tq:] / l[tq:])
    o_ref[...] = (_rms(o) * g_ref[...] * (1.0 - lam_init)).astype(BF16)


def _gqa_attn_kernel(*refs, G, L, T):
    q_ref = refs[0]
    kv_refs, o_ref = refs[1:-1], refs[-1]
    tq = q_ref.shape[0]
    qs = jnp.concatenate([q_ref[:, g * LANES:(g + 1) * LANES] for g in range(G)], axis=0)
    acc, l = _flash(qs, _kv_sources(kv_refs, L, T))
    o = acc / l
    for g in range(G):
        o_ref[:, g * LANES:(g + 1) * LANES] = o[g * tq:(g + 1) * tq].astype(BF16)


def _attn_call(kernel, plan, q, k, v, kv_col, n_heads, q_width, small, *, tq_lat, context, name):
    B, T, L = plan.B, plan.T, plan.L
    ctx_row = plan.n_lat // L
    kvw = LANES
    small_specs = [pl.BlockSpec(s.shape, lambda b, h, i: (0, 0)) for s in small]
    kv_specs = [pl.BlockSpec((L, kvw), lambda b, h, i: (ctx_row + b, kv_col(h)))] * 2
    args = list(small) + [q, k, v]
    if context:
        tq, nq, rows = L, 1, plan.n_ctx
        q_spec = pl.BlockSpec((tq, q_width), lambda b, h, i: (ctx_row + b, h))
        o_spec = pl.BlockSpec((tq, q_width), lambda b, h, i: (b, h))
    else:
        tq = min(tq_lat, T)
        nq, rows = T // tq, plan.n_lat
        q_spec = o_spec = pl.BlockSpec((tq, q_width), lambda b, h, i: (b * nq + i, h))
        kv_specs = kv_specs + [pl.BlockSpec((T, kvw), lambda b, h, i: (b, kv_col(h)))] * 2
        args += [k, v]
    return pl.pallas_call(
        kernel,
        out_shape=jax.ShapeDtypeStruct((rows, q.shape[1]), BF16),
        grid=(B, n_heads, nq),
        in_specs=small_specs + [q_spec] + kv_specs,
        out_specs=o_spec,
        compiler_params=_params("arbitrary", "arbitrary", "arbitrary"),
        name=name,
    )(*args)


def _ret_kernel(dec_ref, qc_ref, kc_ref, vc_ref, gc_ref, ql_ref, kl_ref, vl_ref, gl_ref,
                yc_ref, yl_ref, sb_ref, st_ref, *, C, n_chunks):
    lg = jnp.log(1.0 - jnp.exp(-dec_ref[...] * math.log(2.0)))
    lgf, lgb = lg[0:1], lg[1:2]
    i = lax.broadcasted_iota(jnp.int32, (C, 1), 0).astype(F32)
    qdf, kef = jnp.exp((i + 1.0) * lgf), jnp.exp((C - 1.0 - i) * lgf)
    qdb, keb = jnp.exp((C - i) * lgb), jnp.exp(i * lgb)
    gfc, gbc = jnp.exp(C * lgf), jnp.exp(C * lgb)
    dist = (lax.broadcasted_iota(jnp.int32, (C, C), 0) - lax.broadcasted_iota(jnp.int32, (C, C), 1)).astype(F32)
    w = jnp.where(dist > 0, jnp.exp(jnp.maximum(dist, 0.0) * lgf),
                  jnp.where(dist < 0, jnp.exp(jnp.maximum(-dist, 0.0) * lgb), 2.0))

    def intra(q, k, v):
        return _dot((_dot_nt(q, k) * w).astype(BF16), v)

    def kv_state(k, v, ke):
        return _dot_tn((k.astype(F32) * ke).astype(BF16), v)

    def readout(o, g):
        return (g.astype(F32) * _rms(o)).astype(BF16)

    qx, kx, vx = qc_ref[...], kc_ref[...], vc_ref[...]
    yc_ref[...] = readout(intra(qx, kx, vx), gc_ref[...])

    st_ref[...] = kv_state(kx, vx, keb)

    def bwd(t, carry):
        c = n_chunks - 1 - t
        rows = pl.ds(pl.multiple_of(c * C, C), C)
        s = st_ref[...]
        sb_ref[c] = s.astype(BF16)
        st_ref[...] = gbc * s + kv_state(kl_ref[rows, :], vl_ref[rows, :], keb)
        return carry

    lax.fori_loop(0, n_chunks, bwd, 0, unroll=RET_UNROLL)

    st_ref[...] = kv_state(kx, vx, kef)

    def fwd(c, carry):
        rows = pl.ds(pl.multiple_of(c * C, C), C)
        q, k, v = ql_ref[rows, :], kl_ref[rows, :], vl_ref[rows, :]
        qf = q.astype(F32)
        s = st_ref[...]
        o = (intra(q, k, v) + _dot((qf * qdf).astype(BF16), s.astype(BF16))
             + _dot((qf * qdb).astype(BF16), sb_ref[c]))
        yl_ref[rows, :] = readout(o, gl_ref[rows, :])
        st_ref[...] = gfc * s + kv_state(k, v, kef)
        return carry

    lax.fori_loop(0, n_chunks, fwd, 0, unroll=RET_UNROLL)


def _ret_call(plan, q, k, v, g, decay_exp):
    B, T, L = plan.B, plan.T, plan.L
    NT, D = q.shape
    H = RET_HEADS
    dk, dv = D // H, v.shape[1] // H
    C = RET_CHUNK
    n_chunks = T // C
    ctx_row = plan.n_lat // L
    dec = jnp.transpose(decay_exp.astype(F32)).reshape(H, 2, 1)
    ctx = lambda w: pl.BlockSpec((L, w), lambda b, h: (ctx_row + b, h))
    lat = lambda w: pl.BlockSpec((T, w), lambda b, h: (b, h))
    yc, yl = pl.pallas_call(
        functools.partial(_ret_kernel, C=C, n_chunks=n_chunks),
        out_shape=[jax.ShapeDtypeStruct((plan.n_ctx, v.shape[1]), BF16),
                   jax.ShapeDtypeStruct((plan.n_lat, v.shape[1]), BF16)],
        grid=(B, H),
        in_specs=[pl.BlockSpec((None, 2, 1), lambda b, h: (h, 0, 0)),
                  ctx(dk), ctx(dk), ctx(dv), ctx(dv), lat(dk), lat(dk), lat(dv), lat(dv)],
        out_specs=[pl.BlockSpec((L, dv), lambda b, h: (b, h)),
                   pl.BlockSpec((T, dv), lambda b, h: (b, h))],
        scratch_shapes=[pltpu.VMEM((n_chunks, dk, dv), BF16), pltpu.VMEM((dk, dv), F32)],
        compiler_params=_params("arbitrary", "arbitrary"),
        name="retention",
    )(dec, q, k, v, g, q, k, v, g)
    return yl, yc


def _hgrn_lower_bound(logits, layer):
    e = jnp.exp(logits - jnp.max(logits, axis=0, keepdims=True))
    p = e / jnp.sum(e, axis=0, keepdims=True)
    return jnp.sum(p[0:layer + 1], axis=0, keepdims=True) - p[0:1]


def _hgrn_kernel(lb_ref, qf_ref, vf_ref, zf_ref, qb_ref, vb_ref, zb_ref, of_ref, ob_ref,
                 sf_ref, sb_ref, *, layer, C, n_heads):
    @pl.when(pl.program_id(1) == 0)
    def _():
        sf_ref[...] = jnp.zeros_like(sf_ref)
        sb_ref[...] = jnp.zeros_like(sb_ref)

    lb = _hgrn_lower_bound(lb_ref[...], layer)
    TB, D = qf_ref.shape
    n_sub = TB // C
    r_i = lax.broadcasted_iota(jnp.int32, (TB, TB), 0)
    c_i = lax.broadcasted_iota(jnp.int32, (TB, TB), 1)
    same_chunk = (r_i // C) == (c_i // C)
    m_r = lax.broadcasted_iota(jnp.int32, (C, C), 0)
    m_c = lax.broadcasted_iota(jnp.int32, (C, C), 1)

    def direction(q_ref, v_ref, z_ref, o_ref, st_ref, causal, mask, mid, last, order):
        f = lb + (1.0 - lb) * jax.nn.sigmoid(z_ref[...])
        kk = 1.0 - f
        la = jnp.log(f)
        tri = jnp.where(same_chunk & causal, 1.0, 0.0).astype(BF16)
        hi = la.astype(BF16)
        lo = (la - hi.astype(F32)).astype(BF16)
        b = _dot(tri, hi) + _dot(tri, lo)
        refs = [b[j * C + mid:j * C + mid + 1] for j in range(n_sub)]
        tots = [b[j * C + last:j * C + last + 1] for j in range(n_sub)]
        ref_rows = jnp.concatenate([jnp.broadcast_to(r, (C, D)) for r in refs], axis=0)
        dl = (b - ref_rows) * LOG2E
        qd = q_ref[...].astype(F32) * jnp.exp2(dl)
        kd = kk * jnp.exp2(-dl)
        v = v_ref[...]
        for j in order:
            rows = slice(j * C, (j + 1) * C)
            qd_j, kd_j = qd[rows], kd[rows]
            qdb, kdb = qd_j.astype(BF16), kd_j.astype(BF16)
            qs = (qd_j * jnp.exp(refs[j])).astype(BF16)
            ke = (kd_j * jnp.exp(tots[j] - refs[j])).astype(BF16)
            dec = jnp.exp(tots[j])
            for h in range(n_heads):
                sl = slice(h * HGRN_HEAD_DIM, (h + 1) * HGRN_HEAD_DIM)
                att = jnp.where(mask, _dot_nt(qdb[:, sl], kdb[:, sl]), 0.0).astype(BF16)
                st = st_ref[h]
                o_ref[rows, sl] = _dot(att, v[rows, sl]) + _dot_nt(qs[:, sl], st.astype(BF16))
                st_ref[h] = st * dec[:, sl] + _dot_tn(v[rows, sl], ke[:, sl])

    direction(qf_ref, vf_ref, zf_ref, of_ref, sf_ref, r_i >= c_i, m_r >= m_c, C // 2 - 1, C - 1,
              range(n_sub))
    direction(qb_ref, vb_ref, zb_ref, ob_ref, sb_ref, r_i <= c_i, m_r <= m_c, C // 2, 0,
              range(n_sub - 1, -1, -1))


def _hgrn_call(plan, layer, lb_logits, q, v, zf, zb):
    B, T, L = plan.B, plan.T, plan.L
    NT, D = q.shape
    TB = HGRN_BLOCK
    nb = T // TB
    ctx_blk = plan.n_lat // TB
    H = D // HGRN_HEAD_DIM
    fwd = lambda b, s: (jnp.where(s == 0, ctx_blk + b, b * nb + s - 1), 0)
    bwd = lambda b, s: (jnp.where(s == 0, ctx_blk + b, b * nb + nb - s), 0)
    blk = lambda im: pl.BlockSpec((TB, D), im)
    return pl.pallas_call(
        functools.partial(_hgrn_kernel, layer=layer, C=HGRN_CHUNK, n_heads=H),
        out_shape=[jax.ShapeDtypeStruct((NT, D), F32), jax.ShapeDtypeStruct((NT, D), F32)],
        grid=(B, nb + 1),
        in_specs=[pl.BlockSpec(lb_logits.shape, lambda b, s: (0, 0)),
                  blk(fwd), blk(fwd), blk(fwd), blk(bwd), blk(bwd), blk(bwd)],
        out_specs=[blk(fwd), blk(bwd)],
        scratch_shapes=[pltpu.VMEM((H, HGRN_HEAD_DIM, HGRN_HEAD_DIM), F32),
                        pltpu.VMEM((H, HGRN_HEAD_DIM, HGRN_HEAD_DIM), F32)],
        compiler_params=_params("arbitrary", "arbitrary"),
        name="hgrn2",
    )(lb_logits, q, v, zf, q, v, zb)


def _oproj_kernel(h_ref, mod_ref, y_ref, w_ref, o_ref):
    o_ref[...] = h_ref[...] + mod_ref[5:6, :] * _dot(y_ref[...], w_ref[...])


def _oproj_split_kernel(h_ref, mod_ref, yl_ref, yc_ref, w_ref, o_ref, *, lat_tiles):
    i = pl.program_id(0)

    @pl.when(i < lat_tiles)
    def _():
        o_ref[...] = h_ref[...] + mod_ref[5:6, :] * _dot(yl_ref[...], w_ref[...])

    @pl.when(i >= lat_tiles)
    def _():
        o_ref[...] = h_ref[...] + mod_ref[5:6, :] * _dot(yc_ref[...], w_ref[...])


def _oproj_hgrn_kernel(h_ref, mod_ref, of_ref, ob_ref, g_ref, ng_ref, w_ref, o_ref):
    o = of_ref[...] + ob_ref[...]
    ng = ng_ref[...]
    ys = []
    for n in range(0, o.shape[1], HGRN_HEAD_DIM):
        sl = slice(n, n + HGRN_HEAD_DIM)
        ys.append((_rms(o[:, sl]) * ng * g_ref[:, sl].astype(F32)).astype(BF16))
    y = jnp.concatenate(ys, axis=1)
    o_ref[...] = h_ref[...] + mod_ref[5:6, :] * _dot(y, w_ref[...])


def _oproj_call(kernel, H, mod, layer, ys, small, w, j, n_tiles, plan, name, y_specs=None):
    NT, D = H.shape
    TM = plan.TM
    if y_specs is None:
        y_specs = [pl.BlockSpec((TM, y.shape[1]), lambda i: (i, 0)) for y in ys]
    in_specs = ([pl.BlockSpec((TM, D), lambda i: (i, 0)),
                 pl.BlockSpec((None, None, N_MOD, D), lambda i: (layer, plan.mod_row(i), 0, 0))]
                + y_specs
                + [pl.BlockSpec(s.shape, lambda i: (0, 0)) for s in small]
                + [_resident((None,) + w.shape[1:], lambda i: (j, 0, 0))])
    return pl.pallas_call(
        kernel,
        out_shape=jax.ShapeDtypeStruct((NT, D), F32),
        grid=(n_tiles,),
        in_specs=in_specs,
        out_specs=pl.BlockSpec((TM, D), lambda i: (i, 0)),
        input_output_aliases={0: 0},
        compiler_params=_params("arbitrary"),
        name=name,
    )(H, mod, *ys, *small, w)


def kernel(x, c, ctx, c_ctx, mod_w, mod_b, ffn1_w13, ffn1_w2, ffn2_w13, ffn2_w2, diff_w_in, diff_w_out, diff_lambda, diff_subln_g, ret_w_in, ret_w_out, ret_decay_exp, hgrn_w_in, hgrn_w_out, hgrn_lb_logits, hgrn_norm_g, gqa_w_in, gqa_w_out, gqa_q_norm_g, gqa_k_norm_g, final_norm_g):
    B, T, D = x.shape
    L = ctx.shape[1]
    depth = mod_w.shape[0]
    plan = _Plan(B, T, L)

    cstack = jnp.concatenate([c, c_ctx[None, :], jnp.zeros((MOD_ROWS - B - 1, D), F32)], axis=0)
    mod = _mod_call(cstack, mod_w, mod_b)
    H = (x.reshape(B * T, D), ctx.reshape(B * L, D))
    bf = lambda w: w.astype(BF16)
    w13_cur, w2_cur = bf(ffn1_w13[0:1]), bf(ffn1_w2[0:1])
    lat = plan.lat_tiles

    for i in range(depth):
        kind, j = i % N_MIXERS, i // N_MIXERS
        need_ctx = i < depth - 1
        mix_tiles = plan.all_tiles if need_ctx else plan.lat_tiles

        if i == 0:
            H = _ffn_call(H, mod, i, 0, w13_cur, w2_cur, 0, plan.all_tiles, plan)
            w13_cur, w2_cur = bf(ffn2_w13[0:1]), bf(ffn2_w2[0:1])
        else:
            H, w13_cur, w2_cur = _ffn_call(H, mod, i, 0, w13_cur, w2_cur, 0, plan.all_tiles, plan,
                                           cast_next=(ffn2_w13, ffn2_w2, i))

        def out_proj(H, yl, yc, w, name):
            if yc is None:
                return _oproj_call(_oproj_kernel, H, mod, i, [yl], [], w, j, lat, plan, name)
            width = yl.shape[1]
            y_specs = [pl.BlockSpec((plan.TM, width), lambda t: (jnp.minimum(t, lat - 1), 0)),
                       pl.BlockSpec((plan.TM, width), lambda t: (jnp.maximum(t - lat, 0), 0))]
            return _oproj_call(functools.partial(_oproj_split_kernel, lat_tiles=lat), H, mod, i, [yl, yc], [],
                               w, j, plan.all_tiles, plan, name, y_specs=y_specs)

        if kind == 0:
            d = D // DIFF_HEADS // 2
            lam_init = 0.8 - 0.6 * math.exp(-0.3 * i)
            tabs = _rope_tables(plan, d, LANES)
            w_in = bf(diff_w_in)
            w_in = jnp.concatenate([_pair_halves(w_in[..., :D], d), _pair_halves(w_in[..., D:2 * D], d),
                                    w_in[..., 2 * D:]], axis=-1)
            q, k, v = _proj_call(
                functools.partial(_proj_diff_kernel, D=D, d=d), H, mod, i, w_in, j,
                tabs, _table_specs(plan, tabs), [(D, BF16)] * 3, plan, "proj_diff")
            small = [diff_lambda[j], diff_subln_g[j].reshape(1, 2 * d)]
            kern = functools.partial(_diff_attn_kernel, lam_init=lam_init, L=L, T=T)
            attend = functools.partial(_attn_call, kern, plan, q, k, v, lambda h: h, DIFF_HEADS, LANES, small,
                                       tq_lat=DIFF_TQ)
            yl = attend(context=False, name="diff_attn")
            yc = attend(context=True, name="diff_attn_ctx") if need_ctx else None
            H = out_proj(H, yl, yc, bf(diff_w_out), "oproj_diff")
        elif kind == 1:
            dk = D // RET_HEADS
            tabs = _rope_tables(plan, dk, dk)
            q, k, v, g = _proj_call(
                functools.partial(_proj_ret_kernel, D=D, dk=dk), H, mod, i, bf(ret_w_in), j,
                tabs, _table_specs(plan, tabs), [(D, BF16), (D, BF16), (2 * D, BF16), (2 * D, BF16)],
                plan, "proj_ret")
            yl, yc = _ret_call(plan, q, k, v, g, ret_decay_exp[j])
            H = out_proj(H, yl, yc if need_ctx else None, bf(ret_w_out), "oproj_ret")
        elif kind == 2:
            q, v, g, zf, zb = _proj_call(
                functools.partial(_proj_hgrn_kernel, D=D), H, mod, i, bf(hgrn_w_in), j,
                [], [], [(D, BF16), (D, BF16), (D, BF16), (D, F32), (D, F32)], plan, "proj_hgrn")
            o_f, o_b = _hgrn_call(plan, i, hgrn_lb_logits, q, v, zf, zb)
            H = _oproj_call(_oproj_hgrn_kernel, H, mod, i, [o_f, o_b, g],
                            [hgrn_norm_g[j].reshape(1, HGRN_HEAD_DIM)], bf(hgrn_w_out), j,
                            mix_tiles, plan, "oproj_hgrn")
        else:
            d = GQA_HEAD_DIM
            kvw = GQA_KV_HEADS * d
            G = D // d // GQA_KV_HEADS
            tabs = _rope_tables(plan, d, LANES)
            extra = list(tabs) + [_pair_halves(gqa_q_norm_g[j].reshape(1, d), d),
                                  _pair_halves(gqa_k_norm_g[j].reshape(1, d), d)]
            specs = _table_specs(plan, tabs) + [pl.BlockSpec((1, d), lambda t: (0, 0))] * 2
            w_in = bf(gqa_w_in)
            w_in = jnp.concatenate([_pair_halves(w_in[..., :D], d), _pair_halves(w_in[..., D:D + kvw], d),
                                    w_in[..., D + kvw:]], axis=-1)
            q, k, v = _proj_call(
                functools.partial(_proj_gqa_kernel, D=D, d=d, kvw=kvw), H, mod, i, w_in, j,
                extra, specs, [(D, BF16), (kvw, BF16), (kvw, BF16)], plan, "proj_gqa")
            kern = functools.partial(_gqa_attn_kernel, G=G, L=L, T=T)
            attend = functools.partial(_attn_call, kern, plan, q, k, v, lambda h: h, GQA_KV_HEADS, G * d, [],
                                       tq_lat=GQA_TQ)
            yl = attend(context=False, name="gqa_attn")
            yc = attend(context=True, name="gqa_attn_ctx") if need_ctx else None
            H = out_proj(H, yl, yc, bf(gqa_w_out), "oproj_gqa")

        if need_ctx:
            H, w13_cur, w2_cur = _ffn_call(H, mod, i, 6, w13_cur, w2_cur, 0, plan.all_tiles, plan,
                                           cast_next=(ffn1_w13, ffn1_w2, i + 1))
        else:
            H = _ffn_call(H, mod, i, 6, w13_cur, w2_cur, 0, plan.lat_tiles, plan, final_g=final_norm_g)
    return H.reshape(B, T, D)
```

```python
import functools
import math

import jax
import jax.numpy as jnp
import numpy as np
from jax import lax
from jax.experimental import pallas as pl
from jax.experimental.pallas import tpu as pltpu

F32 = jnp.float32
BF16 = jnp.bfloat16

NORM_EPS = 1e-6
ROPE_THETA = 10000.0
GRID_W = 64
N_MOD = 9
N_MIXERS = 4
LOG2E = math.log2(math.e)

LANES = 128
MOD_ROWS = 8
VMEM_LIMIT = 56 * 1024 * 1024

DIFF_HEADS = 8
RET_HEADS = 4
GQA_KV_HEADS = 2
GQA_HEAD_DIM = 128
HGRN_HEAD_DIM = 128
FFN_CHUNK = 256
RET_CHUNK = 256
RET_UNROLL = 4
HGRN_CHUNK = 128
HGRN_BLOCK = 256
ATTN_KV_CHUNK = 256
DIFF_TQ = 2048
GQA_TQ = 1024


def _params(*sem):
    return pltpu.CompilerParams(dimension_semantics=sem, vmem_limit_bytes=VMEM_LIMIT)


def _dot(a, b):
    return jnp.dot(a, b, preferred_element_type=F32)


def _dot_nt(a, b):
    return lax.dot_general(a, b, (((1,), (1,)), ((), ())), preferred_element_type=F32)


def _dot_tn(a, b):
    return lax.dot_general(a, b, (((0,), (0,)), ((), ())), preferred_element_type=F32)


def _silu(x):
    return x * jax.nn.sigmoid(x)


def _rms(x):
    return x * lax.rsqrt(jnp.mean(x * x, axis=-1, keepdims=True) + NORM_EPS)


def _rms_block(y):
    sq = y * y
    hi = sq.astype(BF16)
    lo = (sq - hi.astype(F32)).astype(BF16)
    ones = jnp.ones((LANES, LANES), BF16)
    ms = (_dot(hi, ones) + _dot(lo, ones)) * (1.0 / LANES)
    return y * lax.rsqrt(ms + NORM_EPS)


def _modulated(h, mod_ref, k0):
    shift = mod_ref[k0:k0 + 1, :]
    scale = mod_ref[k0 + 1:k0 + 2, :]
    return _rms(h) * (1.0 + scale) + shift


HALF = LANES // 2


def _rope(y, c, s):
    return y * c + pltpu.roll(y, HALF, 1) * s


def _pair_halves(w, d):
    nq = d // 4
    lead = w.shape[:-1]
    x = w.reshape(*lead, -1, LANES // d, 2, 2, nq)
    return jnp.moveaxis(x, -2, -4).reshape(*lead, -1)


class _Plan:
    def __init__(self, B, T, L):
        self.B, self.T, self.L = B, T, L
        self.n_lat = B * T
        self.n_ctx = B * L
        self.NT = self.n_lat + self.n_ctx
        for tm in (1024, 512, 256):
            if T % tm == 0 and self.n_ctx % tm == 0:
                self.TM = tm
                break
        else:
            raise ValueError("unsupported sequence lengths")
        assert L == RET_CHUNK == HGRN_BLOCK and T % L == 0 and T % GRID_W == 0
        assert B + 1 <= MOD_ROWS
        self.tiles_per_batch = T // self.TM
        self.lat_tiles = self.n_lat // self.TM
        self.all_tiles = self.NT // self.TM

    def mod_row(self, i):
        return jnp.where(i < self.lat_tiles, i // self.tiles_per_batch, self.B)

    def pos_block(self, i):
        return jnp.where(i < self.lat_tiles, i % self.tiles_per_batch, self.tiles_per_batch)


def _rope_tables(plan, d, width):
    T = plan.T
    rows = T // GRID_W
    row = np.repeat(np.arange(rows, dtype=np.float32), GRID_W)
    col = np.tile(np.arange(GRID_W, dtype=np.float32), rows)
    nq = d // 4
    inv = (np.float32(ROPE_THETA) ** (-np.arange(nq, dtype=np.float32) * np.float32(2.0) / np.float32(d // 2)))
    inv = inv.astype(np.float32)
    ar, ac = row[:, None] * inv, col[:, None] * inv
    cr, sr, cc, sc = np.cos(ar), np.sin(ar), np.cos(ac), np.sin(ac)
    if 2 * nq == LANES:
        c = np.concatenate([cr, cr, cc, cc], axis=1)
        s = np.concatenate([-sr, sr, -sc, sc], axis=1)
    else:
        units = LANES // d
        c = np.concatenate([cr, cc] * (2 * units), axis=1)
        s = np.concatenate([-sr, -sc] * units + [sr, sc] * units, axis=1)
    assert c.shape[1] == width

    def finish(t, fill):
        t = np.concatenate([t, np.full((plan.TM, width), fill, np.float32)], axis=0)
        return jnp.asarray(t.astype(np.float32))

    return finish(c, 1.0), finish(s, 0.0)


def _mod_kernel(c_ref, w_ref, b_ref, o_ref):
    cond = _silu(c_ref[...]).astype(BF16)
    o_ref[...] = _dot(cond, w_ref[...].astype(BF16)) + b_ref[...]


def _mod_call(cstack, mod_w, mod_b):
    depth, D, _ = mod_w.shape
    out = pl.pallas_call(
        _mod_kernel,
        out_shape=jax.ShapeDtypeStruct((depth, MOD_ROWS, N_MOD * D), F32),
        grid=(depth, N_MOD),
        in_specs=[
            pl.BlockSpec((MOD_ROWS, D), lambda l, n: (0, 0)),
            pl.BlockSpec((None, D, D), lambda l, n: (l, 0, n)),
            pl.BlockSpec((None, 1, D), lambda l, n: (l, 0, n)),
        ],
        out_specs=pl.BlockSpec((None, MOD_ROWS, D), lambda l, n: (l, 0, n)),
        compiler_params=_params("arbitrary", "arbitrary"),
        name="mod_table",
    )(cstack, mod_w, mod_b.reshape(depth, 1, N_MOD * D))
    return out.reshape(depth, MOD_ROWS, N_MOD, D)


def _resident(shape, index_map):
    return pl.BlockSpec(shape, index_map, pipeline_mode=pl.Buffered(1))


BF16_ROWS = 16


def _cast_steps(n_tiles, rows):
    for c in range(n_tiles, 0, -1):
        if rows % (c * BF16_ROWS) == 0:
            return c
    raise ValueError("weights cannot be split into aligned slabs")


def _cast_plumbing(n_tiles, jobs):
    in_specs, args, out_shape, out_specs = [], [], [], []
    for arr, idx in jobs:
        _, R, C = arr.shape
        steps = _cast_steps(n_tiles, R)
        in_specs.append(pl.BlockSpec((None, R // steps, C),
                                     lambda i, idx=idx, steps=steps: (idx, jnp.minimum(i, steps - 1), 0)))
        args.append(arr)
        out_shape.append(jax.ShapeDtypeStruct((1, R, C), BF16))
        out_specs.append(pl.BlockSpec((None, R // steps, C),
                                      lambda i, steps=steps: (0, jnp.minimum(i, steps - 1), 0)))
    return in_specs, args, out_shape, out_specs


def _with_casts(kernel, n_in, n_out, n_jobs):
    if n_jobs == 0:
        return kernel

    def wrapped(*refs):
        ins, cast_in = refs[:n_in], refs[n_in:n_in + n_jobs]
        outs = refs[n_in + n_jobs:n_in + n_jobs + n_out]
        cast_out = refs[n_in + n_jobs + n_out:]
        for src, dst in zip(cast_in, cast_out):
            dst[...] = src[...].astype(BF16)
        kernel(*ins, *outs)

    return wrapped


def _ffn_kernel(*refs, k0, F, lat_tiles, split_in, final):
    refs = list(refs)
    if split_in:
        hl_ref, hc_ref = refs[:2]
        x = jnp.where(pl.program_id(0) < lat_tiles, hl_ref[...], hc_ref[...])
        refs = refs[2:]
    else:
        x = refs[0][...]
        refs = refs[1:]
    mod_ref, w13_ref, w2_ref = refs[:3]
    o_ref = refs[-1]
    xn = _modulated(x, mod_ref, k0).astype(BF16)
    acc = None
    for lo in range(0, F, FFN_CHUNK):
        a = _dot(xn, w13_ref[:, lo:lo + FFN_CHUNK])
        b = _dot(xn, w13_ref[:, F + lo:F + lo + FFN_CHUNK])
        part = _dot((_silu(a) * b).astype(BF16), w2_ref[lo:lo + FFN_CHUNK, :])
        acc = part if acc is None else acc + part
    hn = x + 0.5 * mod_ref[k0 + 2:k0 + 3, :] * acc
    if final:
        hn = _rms(hn) * refs[3][...]
    o_ref[...] = hn


def _ffn_call(h_in, mod, layer, k0, w13, w2, widx, n_tiles, plan, final_g=None, casts=()):
    split_in = isinstance(h_in, tuple)
    D, F = w2.shape[2], w2.shape[1]
    TM = plan.TM
    lat = plan.lat_tiles
    final = final_g is not None
    if split_in:
        in_specs = [pl.BlockSpec((TM, D), lambda i: (jnp.minimum(i, lat - 1), 0)),
                    pl.BlockSpec((TM, D), lambda i: (jnp.maximum(i - lat, 0), 0))]
        args = list(h_in)
    else:
        in_specs = [pl.BlockSpec((TM, D), lambda i: (i, 0))]
        args = [h_in]
    in_specs += [
        pl.BlockSpec((None, None, N_MOD, D), lambda i: (layer, plan.mod_row(i), 0, 0)),
        _resident((None, D, 2 * F), lambda i: (widx, 0, 0)),
        _resident((None, F, D), lambda i: (widx, 0, 0)),
    ]
    args += [mod, w13, w2]
    if final:
        in_specs.append(pl.BlockSpec((1, D), lambda i: (0, 0)))
        args.append(final_g.reshape(1, D))
    aliases = {} if (final or split_in) else {0: 0}
    rows = n_tiles * TM if final else plan.NT
    c_in, c_args, c_shape, c_out = _cast_plumbing(n_tiles, casts)
    kern = functools.partial(_ffn_kernel, k0=k0, F=F, lat_tiles=lat, split_in=split_in, final=final)
    return pl.pallas_call(
        _with_casts(kern, len(args), 1, len(casts)),
        out_shape=[jax.ShapeDtypeStruct((rows, D), F32)] + c_shape,
        grid=(n_tiles,),
        in_specs=in_specs + c_in,
        out_specs=[pl.BlockSpec((TM, D), lambda i: (i, 0))] + c_out,
        input_output_aliases=aliases,
        compiler_params=_params("arbitrary"),
        name="ffn_final" if final else "ffn",
    )(*args, *c_args)


PROJ_COLS = 256


def _proj_diff_kernel(h_ref, mod_ref, w_ref, c_ref, s_ref, q_ref, k_ref, v_ref, *, D, d):
    xm = _modulated(h_ref[...], mod_ref, 3).astype(BF16)
    c, s = c_ref[...], s_ref[...]
    for n in range(0, D, PROJ_COLS):
        yq = _dot(xm, w_ref[:, n:n + PROJ_COLS]) * (d ** -0.5 * LOG2E)
        yk = _dot(xm, w_ref[:, D + n:D + n + PROJ_COLS])
        for m in range(0, PROJ_COLS, LANES):
            q_ref[:, n + m:n + m + LANES] = _rope(yq[:, m:m + LANES], c, s).astype(BF16)
            k_ref[:, n + m:n + m + LANES] = _rope(yk[:, m:m + LANES], c, s).astype(BF16)
        v_ref[:, n:n + PROJ_COLS] = _dot(xm, w_ref[:, 2 * D + n:2 * D + n + PROJ_COLS]).astype(BF16)


def _proj_ret_kernel(h_ref, mod_ref, w_ref, c_ref, s_ref, q_ref, k_ref, v_ref, g_ref, *, D, dk):
    xm = _modulated(h_ref[...], mod_ref, 3).astype(BF16)
    c, s = c_ref[...], s_ref[...]
    for n in range(0, D, PROJ_COLS):
        yq = _dot(xm, w_ref[:, n:n + PROJ_COLS])
        yk = _dot(xm, w_ref[:, D + n:D + n + PROJ_COLS]) * (dk ** -0.5)
        for m in range(0, PROJ_COLS, LANES):
            cm, sm = c[:, m:m + LANES], s[:, m:m + LANES]
            q_ref[:, n + m:n + m + LANES] = _rope(yq[:, m:m + LANES], cm, sm).astype(BF16)
            k_ref[:, n + m:n + m + LANES] = _rope(yk[:, m:m + LANES], cm, sm).astype(BF16)
    for n in range(0, 2 * D, PROJ_COLS):
        v_ref[:, n:n + PROJ_COLS] = _dot(xm, w_ref[:, 2 * D + n:2 * D + n + PROJ_COLS]).astype(BF16)
        g_ref[:, n:n + PROJ_COLS] = _silu(_dot(xm, w_ref[:, 4 * D + n:4 * D + n + PROJ_COLS])).astype(BF16)


def _proj_hgrn_kernel(h_ref, mod_ref, w_ref, q_ref, i_ref, g_ref, zf_ref, zb_ref, *, D):
    xm = _modulated(h_ref[...], mod_ref, 3).astype(BF16)
    for n in range(0, D, PROJ_COLS):
        sl = slice(n, n + PROJ_COLS)
        q_ref[:, sl] = _silu(_dot(xm, w_ref[:, n:n + PROJ_COLS])).astype(BF16)
        i_ref[:, sl] = _dot(xm, w_ref[:, D + n:D + n + PROJ_COLS]).astype(BF16)
        g_ref[:, sl] = _silu(_dot(xm, w_ref[:, 2 * D + n:2 * D + n + PROJ_COLS])).astype(BF16)
        zf_ref[:, sl] = _dot(xm, w_ref[:, 3 * D + n:3 * D + n + PROJ_COLS])
        zb_ref[:, sl] = _dot(xm, w_ref[:, 4 * D + n:4 * D + n + PROJ_COLS])


def _proj_gqa_kernel(h_ref, mod_ref, w_ref, c_ref, s_ref, qg_ref, kg_ref,
                     q_ref, k_ref, v_ref, *, D, d, kvw):
    xm = _modulated(h_ref[...], mod_ref, 3).astype(BF16)
    c, s = c_ref[...], s_ref[...]
    qg = qg_ref[...] * (d ** -0.5 * LOG2E)
    kg = kg_ref[...]
    for n in range(0, D, PROJ_COLS):
        yq = _dot(xm, w_ref[:, n:n + PROJ_COLS])
        for m in range(0, PROJ_COLS, LANES):
            q_ref[:, n + m:n + m + LANES] = _rope(_rms_block(yq[:, m:m + LANES]) * qg, c, s).astype(BF16)
    yk = _dot(xm, w_ref[:, D:D + kvw])
    for m in range(0, kvw, LANES):
        k_ref[:, m:m + LANES] = _rope(_rms_block(yk[:, m:m + LANES]) * kg, c, s).astype(BF16)
    v_ref[...] = _dot(xm, w_ref[:, D + kvw:D + 2 * kvw]).astype(BF16)


def _proj_call(kernel, H, mod, layer, w, j, extra, extra_specs, outs, plan, name, casts=()):
    NT, D = H.shape
    TM = plan.TM
    in_specs = [
        pl.BlockSpec((TM, D), lambda i: (i, 0)),
        pl.BlockSpec((None, None, N_MOD, D), lambda i: (layer, plan.mod_row(i), 0, 0)),
        _resident((None,) + w.shape[1:], lambda i: (j, 0, 0)),
    ] + extra_specs
    args = [H, mod, w, *extra]
    c_in, c_args, c_shape, c_out = _cast_plumbing(plan.all_tiles, casts)
    return pl.pallas_call(
        _with_casts(kernel, len(args), len(outs), len(casts)),
        out_shape=[jax.ShapeDtypeStruct((NT, wd), dt) for wd, dt in outs] + c_shape,
        grid=(plan.all_tiles,),
        in_specs=in_specs + c_in,
        out_specs=[pl.BlockSpec((TM, wd), lambda i: (i, 0)) for wd, _ in outs] + c_out,
        compiler_params=_params("arbitrary"),
        name=name,
    )(*args, *c_args)


def _table_specs(plan, tables):
    return [pl.BlockSpec((plan.TM, t.shape[1]), lambda i: (plan.pos_block(i), 0)) for t in tables]


def _flash(q, sources):
    m = l = acc = None
    for k_ref, v_ref, start, size in sources:
        k = k_ref[start:start + size, :]
        v = v_ref[start:start + size, :]
        s = _dot_nt(q, k)
        tiles = [s[:, t:t + LANES] for t in range(0, size, LANES)]
        mx = functools.reduce(jnp.maximum, tiles)
        ms = jnp.broadcast_to(jnp.max(mx, axis=1, keepdims=True), mx.shape)
        if m is None:
            m = ms
            ps = [jnp.exp2(t - m) for t in tiles]
            l = functools.reduce(jnp.add, ps)
            acc = _dot(jnp.concatenate(ps, axis=1).astype(BF16), v)
        else:
            m_new = jnp.maximum(m, ms)
            alpha = jnp.exp2(m - m_new)
            ps = [jnp.exp2(t - m_new) for t in tiles]
            l = alpha * l + functools.reduce(jnp.add, ps)
            acc = alpha * acc + _dot(jnp.concatenate(ps, axis=1).astype(BF16), v)
            m = m_new
    return acc, jnp.sum(l, axis=1, keepdims=True)


def _kv_sources(kv_refs, L, T):
    if len(kv_refs) == 2:
        kc, vc = kv_refs
        return [(kc, vc, 0, L)]
    kc, vc, kl, vl = kv_refs
    chunk = min(ATTN_KV_CHUNK, T)
    return [(kc, vc, 0, L)] + [(kl, vl, s, chunk) for s in range(0, T, chunk)]


def _diff_attn_kernel(*refs, lam_init, L, T):
    lam_ref, g_ref, q_ref = refs[:3]
    kv_refs, o_ref = refs[3:-1], refs[-1]
    q = q_ref[...]
    tq = q.shape[0]
    lane = lax.broadcasted_iota(jnp.int32, q.shape, 1)
    map0 = (lane & (HALF // 2)) == 0
    zero = jnp.zeros_like(q)
    qs = jnp.concatenate([jnp.where(map0, q, zero), jnp.where(map0, zero, q)], axis=0)
    acc, l = _flash(qs, _kv_sources(kv_refs, L, T))
    lam = lam_ref[...]
    lam_full = (jnp.exp(jnp.sum(lam[0:1] * lam[1:2], axis=1, keepdims=True))
                - jnp.exp(jnp.sum(lam[2:3] * lam[3:4], axis=1, keepdims=True)) + lam_init)
    o = acc[:tq] / l[:tq] - lam_full * (acc[tq:] / l[tq:])
    o_ref[...] = (_rms(o) * g_ref[...] * (1.0 - lam_init)).astype(BF16)


def _gqa_attn_kernel(*refs, G, L, T):
    q_ref = refs[0]
    kv_refs, o_ref = refs[1:-1], refs[-1]
    tq = q_ref.shape[0]
    qs = jnp.concatenate([q_ref[:, g * LANES:(g + 1) * LANES] for g in range(G)], axis=0)
    acc, l = _flash(qs, _kv_sources(kv_refs, L, T))
    o = acc / l
    for g in range(G):
        o_ref[:, g * LANES:(g + 1) * LANES] = o[g * tq:(g + 1) * tq].astype(BF16)


def _attn_call(kernel, plan, q, k, v, kv_col, n_heads, q_width, small, *, tq_lat, context, name):
    B, T, L = plan.B, plan.T, plan.L
    ctx_row = plan.n_lat // L
    kvw = LANES
    small_specs = [pl.BlockSpec(s.shape, lambda b, h, i: (0, 0)) for s in small]
    kv_specs = [pl.BlockSpec((L, kvw), lambda b, h, i: (ctx_row + b, kv_col(h)))] * 2
    args = list(small) + [q, k, v]
    if context:
        tq, nq, rows = L, 1, plan.n_ctx
        q_spec = pl.BlockSpec((tq, q_width), lambda b, h, i: (ctx_row + b, h))
        o_spec = pl.BlockSpec((tq, q_width), lambda b, h, i: (b, h))
    else:
        tq = min(tq_lat, T)
        nq, rows = T // tq, plan.n_lat
        q_spec = o_spec = pl.BlockSpec((tq, q_width), lambda b, h, i: (b * nq + i, h))
        kv_specs = kv_specs + [pl.BlockSpec((T, kvw), lambda b, h, i: (b, kv_col(h)))] * 2
        args += [k, v]
    return pl.pallas_call(
        kernel,
        out_shape=jax.ShapeDtypeStruct((rows, q.shape[1]), BF16),
        grid=(B, n_heads, nq),
        in_specs=small_specs + [q_spec] + kv_specs,
        out_specs=o_spec,
        compiler_params=_params("arbitrary", "arbitrary", "arbitrary"),
        name=name,
    )(*args)


def _ret_kernel(dec_ref, qc_ref, kc_ref, vc_ref, gc_ref, ql_ref, kl_ref, vl_ref, gl_ref,
                yc_ref, yl_ref, sb_ref, st_ref, *, C, n_chunks):
    lg = jnp.log(1.0 - jnp.exp(-dec_ref[...] * math.log(2.0)))
    lgf, lgb = lg[0:1], lg[1:2]
    i = lax.broadcasted_iota(jnp.int32, (C, 1), 0).astype(F32)
    qdf, kef = jnp.exp((i + 1.0) * lgf), jnp.exp((C - 1.0 - i) * lgf)
    qdb, keb = jnp.exp((C - i) * lgb), jnp.exp(i * lgb)
    gfc, gbc = jnp.exp(C * lgf), jnp.exp(C * lgb)
    dist = (lax.broadcasted_iota(jnp.int32, (C, C), 0) - lax.broadcasted_iota(jnp.int32, (C, C), 1)).astype(F32)
    w = jnp.where(dist > 0, jnp.exp(jnp.maximum(dist, 0.0) * lgf),
                  jnp.where(dist < 0, jnp.exp(jnp.maximum(-dist, 0.0) * lgb), 2.0))

    def intra(q, k, v):
        return _dot((_dot_nt(q, k) * w).astype(BF16), v)

    def kv_state(k, v, ke):
        return _dot_tn((k.astype(F32) * ke).astype(BF16), v)

    def readout(o, g):
        return (g.astype(F32) * _rms(o)).astype(BF16)

    qx, kx, vx = qc_ref[...], kc_ref[...], vc_ref[...]
    yc_ref[...] = readout(intra(qx, kx, vx), gc_ref[...])

    st_ref[...] = kv_state(kx, vx, keb)

    def bwd(t, carry):
        c = n_chunks - 1 - t
        rows = pl.ds(pl.multiple_of(c * C, C), C)
        s = st_ref[...]
        sb_ref[c] = s.astype(BF16)
        st_ref[...] = gbc * s + kv_state(kl_ref[rows, :], vl_ref[rows, :], keb)
        return carry

    lax.fori_loop(0, n_chunks, bwd, 0, unroll=RET_UNROLL)

    st_ref[...] = kv_state(kx, vx, kef)

    def fwd(c, carry):
        rows = pl.ds(pl.multiple_of(c * C, C), C)
        q, k, v = ql_ref[rows, :], kl_ref[rows, :], vl_ref[rows, :]
        qf = q.astype(F32)
        s = st_ref[...]
        o = (intra(q, k, v) + _dot((qf * qdf).astype(BF16), s.astype(BF16))
             + _dot((qf * qdb).astype(BF16), sb_ref[c]))
        yl_ref[rows, :] = readout(o, gl_ref[rows, :])
        st_ref[...] = gfc * s + kv_state(k, v, kef)
        return carry

    lax.fori_loop(0, n_chunks, fwd, 0, unroll=RET_UNROLL)


def _ret_call(plan, q, k, v, g, decay_exp):
    B, T, L = plan.B, plan.T, plan.L
    NT, D = q.shape
    H = RET_HEADS
    dk, dv = D // H, v.shape[1] // H
    C = RET_CHUNK
    n_chunks = T // C
    ctx_row = plan.n_lat // L
    dec = jnp.transpose(decay_exp.astype(F32)).reshape(H, 2, 1)
    ctx = lambda w: pl.BlockSpec((L, w), lambda b, h: (ctx_row + b, h))
    lat = lambda w: pl.BlockSpec((T, w), lambda b, h: (b, h))
    yc, yl = pl.pallas_call(
        functools.partial(_ret_kernel, C=C, n_chunks=n_chunks),
        out_shape=[jax.ShapeDtypeStruct((plan.n_ctx, v.shape[1]), BF16),
                   jax.ShapeDtypeStruct((plan.n_lat, v.shape[1]), BF16)],
        grid=(B, H),
        in_specs=[pl.BlockSpec((None, 2, 1), lambda b, h: (h, 0, 0)),
                  ctx(dk), ctx(dk), ctx(dv), ctx(dv), lat(dk), lat(dk), lat(dv), lat(dv)],
        out_specs=[pl.BlockSpec((L, dv), lambda b, h: (b, h)),
                   pl.BlockSpec((T, dv), lambda b, h: (b, h))],
        scratch_shapes=[pltpu.VMEM((n_chunks, dk, dv), BF16), pltpu.VMEM((dk, dv), F32)],
        compiler_params=_params("arbitrary", "arbitrary"),
        name="retention",
    )(dec, q, k, v, g, q, k, v, g)
    return yl, yc


def _hgrn_lower_bound(logits, layer):
    e = jnp.exp(logits - jnp.max(logits, axis=0, keepdims=True))
    p = e / jnp.sum(e, axis=0, keepdims=True)
    return jnp.sum(p[0:layer + 1], axis=0, keepdims=True) - p[0:1]


def _hgrn_kernel(lb_ref, qf_ref, vf_ref, zf_ref, qb_ref, vb_ref, zb_ref, of_ref, ob_ref,
                 sf_ref, sb_ref, *, layer, C, n_heads):
    @pl.when(pl.program_id(1) == 0)
    def _():
        sf_ref[...] = jnp.zeros_like(sf_ref)
        sb_ref[...] = jnp.zeros_like(sb_ref)

    lb = _hgrn_lower_bound(lb_ref[...], layer)
    TB, D = qf_ref.shape
    n_sub = TB // C
    r_i = lax.broadcasted_iota(jnp.int32, (TB, TB), 0)
    c_i = lax.broadcasted_iota(jnp.int32, (TB, TB), 1)
    same_chunk = (r_i // C) == (c_i // C)
    m_r = lax.broadcasted_iota(jnp.int32, (C, C), 0)
    m_c = lax.broadcasted_iota(jnp.int32, (C, C), 1)

    def direction(q_ref, v_ref, z_ref, o_ref, st_ref, causal, mask, mid, last, order):
        f = lb + (1.0 - lb) * jax.nn.sigmoid(z_ref[...])
        kk = 1.0 - f
        la = jnp.log(f)
        tri = jnp.where(same_chunk & causal, 1.0, 0.0).astype(BF16)
        hi = la.astype(BF16)
        lo = (la - hi.astype(F32)).astype(BF16)
        b = _dot(tri, hi) + _dot(tri, lo)
        refs = [b[j * C + mid:j * C + mid + 1] for j in range(n_sub)]
        tots = [b[j * C + last:j * C + last + 1] for j in range(n_sub)]
        ref_rows = jnp.concatenate([jnp.broadcast_to(r, (C, D)) for r in refs], axis=0)
        dl = (b - ref_rows) * LOG2E
        qd = q_ref[...].astype(F32) * jnp.exp2(dl)
        kd = kk * jnp.exp2(-dl)
        v = v_ref[...]
        for j in order:
            rows = slice(j * C, (j + 1) * C)
            qd_j, kd_j = qd[rows], kd[rows]
            qdb, kdb = qd_j.astype(BF16), kd_j.astype(BF16)
            qs = (qd_j * jnp.exp(refs[j])).astype(BF16)
            ke = (kd_j * jnp.exp(tots[j] - refs[j])).astype(BF16)
            dec = jnp.exp(tots[j])
            for h in range(n_heads):
                sl = slice(h * HGRN_HEAD_DIM, (h + 1) * HGRN_HEAD_DIM)
                att = jnp.where(mask, _dot_nt(qdb[:, sl], kdb[:, sl]), 0.0).astype(BF16)
                st = st_ref[h]
                o = _dot(att, v[rows, sl]) + _dot_nt(qs[:, sl], st.astype(BF16))
                o_ref[rows, sl] = o.astype(o_ref.dtype)
                st_ref[h] = st * dec[:, sl] + _dot_tn(v[rows, sl], ke[:, sl])

    direction(qf_ref, vf_ref, zf_ref, of_ref, sf_ref, r_i >= c_i, m_r >= m_c, C // 2 - 1, C - 1,
              range(n_sub))
    direction(qb_ref, vb_ref, zb_ref, ob_ref, sb_ref, r_i <= c_i, m_r <= m_c, C // 2, 0,
              range(n_sub - 1, -1, -1))


def _hgrn_call(plan, layer, lb_logits, q, v, zf, zb):
    B, T, L = plan.B, plan.T, plan.L
    NT, D = q.shape
    TB = HGRN_BLOCK
    nb = T // TB
    ctx_blk = plan.n_lat // TB
    H = D // HGRN_HEAD_DIM
    fwd = lambda b, s: (jnp.where(s == 0, ctx_blk + b, b * nb + s - 1), 0)
    bwd = lambda b, s: (jnp.where(s == 0, ctx_blk + b, b * nb + nb - s), 0)
    blk = lambda im: pl.BlockSpec((TB, D), im)
    return pl.pallas_call(
        functools.partial(_hgrn_kernel, layer=layer, C=HGRN_CHUNK, n_heads=H),
        out_shape=[jax.ShapeDtypeStruct((NT, D), BF16), jax.ShapeDtypeStruct((NT, D), BF16)],
        grid=(B, nb + 1),
        in_specs=[pl.BlockSpec(lb_logits.shape, lambda b, s: (0, 0)),
                  blk(fwd), blk(fwd), blk(fwd), blk(bwd), blk(bwd), blk(bwd)],
        out_specs=[blk(fwd), blk(bwd)],
        scratch_shapes=[pltpu.VMEM((H, HGRN_HEAD_DIM, HGRN_HEAD_DIM), F32),
                        pltpu.VMEM((H, HGRN_HEAD_DIM, HGRN_HEAD_DIM), F32)],
        compiler_params=_params("arbitrary", "arbitrary"),
        name="hgrn2",
    )(lb_logits, q, v, zf, q, v, zb)


def _oproj_kernel(h_ref, mod_ref, y_ref, w_ref, o_ref):
    o_ref[...] = h_ref[...] + mod_ref[5:6, :] * _dot(y_ref[...], w_ref[...])


def _oproj_split_kernel(h_ref, mod_ref, yl_ref, yc_ref, w_ref, o_ref, *, lat_tiles):
    i = pl.program_id(0)

    @pl.when(i < lat_tiles)
    def _():
        o_ref[...] = h_ref[...] + mod_ref[5:6, :] * _dot(yl_ref[...], w_ref[...])

    @pl.when(i >= lat_tiles)
    def _():
        o_ref[...] = h_ref[...] + mod_ref[5:6, :] * _dot(yc_ref[...], w_ref[...])


def _oproj_hgrn_kernel(h_ref, mod_ref, of_ref, ob_ref, g_ref, ng_ref, w_ref, o_ref):
    o = of_ref[...].astype(F32) + ob_ref[...].astype(F32)
    ng = ng_ref[...]
    ys = []
    for n in range(0, o.shape[1], HGRN_HEAD_DIM):
        sl = slice(n, n + HGRN_HEAD_DIM)
        ys.append((_rms(o[:, sl]) * ng * g_ref[:, sl].astype(F32)).astype(BF16))
    y = jnp.concatenate(ys, axis=1)
    o_ref[...] = h_ref[...] + mod_ref[5:6, :] * _dot(y, w_ref[...])


def _oproj_call(kernel, H, mod, layer, ys, small, w, j, n_tiles, plan, name, y_specs=None):
    NT, D = H.shape
    TM = plan.TM
    if y_specs is None:
        y_specs = [pl.BlockSpec((TM, y.shape[1]), lambda i: (i, 0)) for y in ys]
    in_specs = ([pl.BlockSpec((TM, D), lambda i: (i, 0)),
                 pl.BlockSpec((None, None, N_MOD, D), lambda i: (layer, plan.mod_row(i), 0, 0))]
                + y_specs
                + [pl.BlockSpec(s.shape, lambda i: (0, 0)) for s in small]
                + [_resident((None,) + w.shape[1:], lambda i: (j, 0, 0))])
    return pl.pallas_call(
        kernel,
        out_shape=jax.ShapeDtypeStruct((NT, D), F32),
        grid=(n_tiles,),
        in_specs=in_specs,
        out_specs=pl.BlockSpec((TM, D), lambda i: (i, 0)),
        input_output_aliases={0: 0},
        compiler_params=_params("arbitrary"),
        name=name,
    )(H, mod, *ys, *small, w)


def kernel(x, c, ctx, c_ctx, mod_w, mod_b, ffn1_w13, ffn1_w2, ffn2_w13, ffn2_w2, diff_w_in, diff_w_out, diff_lambda, diff_subln_g, ret_w_in, ret_w_out, ret_decay_exp, hgrn_w_in, hgrn_w_out, hgrn_lb_logits, hgrn_norm_g, gqa_w_in, gqa_w_out, gqa_q_norm_g, gqa_k_norm_g, final_norm_g):
    B, T, D = x.shape
    L = ctx.shape[1]
    depth = mod_w.shape[0]
    plan = _Plan(B, T, L)

    cstack = jnp.concatenate([c, c_ctx[None, :], jnp.zeros((MOD_ROWS - B - 1, D), F32)], axis=0)
    mod = _mod_call(cstack, mod_w, mod_b)
    H = (x.reshape(B * T, D), ctx.reshape(B * L, D))
    bf = lambda w: w.astype(BF16)
    w13_cur, w2_cur = bf(ffn1_w13[0:1]), bf(ffn1_w2[0:1])
    mixer_w = [(diff_w_in, diff_w_out), (ret_w_in, ret_w_out), (hgrn_w_in, hgrn_w_out), (gqa_w_in, gqa_w_out)]
    w_in, w_out = bf(mixer_w[0][0][0:1]), bf(mixer_w[0][1][0:1])
    lat = plan.lat_tiles

    for i in range(depth):
        kind, j = i % N_MIXERS, i // N_MIXERS
        need_ctx = i < depth - 1
        mix_tiles = plan.all_tiles if need_ctx else plan.lat_tiles

        if i == 0:
            (H,) = _ffn_call(H, mod, i, 0, w13_cur, w2_cur, 0, plan.all_tiles, plan)
            proj_casts = [(ffn2_w13, i), (ffn2_w2, i)]
        else:
            H, w13_cur, w2_cur = _ffn_call(H, mod, i, 0, w13_cur, w2_cur, 0, plan.all_tiles, plan,
                                           casts=[(ffn2_w13, i), (ffn2_w2, i)])
            proj_casts = []

        def project(kern, w, extra, extra_specs, outs, name):
            res = _proj_call(kern, H, mod, i, w, 0, extra, extra_specs, outs, plan, name, casts=proj_casts)
            return res[:len(outs)], res[len(outs):]

        def out_proj(H, yl, yc, name):
            if yc is None:
                return _oproj_call(_oproj_kernel, H, mod, i, [yl], [], w_out, 0, lat, plan, name)
            width = yl.shape[1]
            y_specs = [pl.BlockSpec((plan.TM, width), lambda t: (jnp.minimum(t, lat - 1), 0)),
                       pl.BlockSpec((plan.TM, width), lambda t: (jnp.maximum(t - lat, 0), 0))]
            return _oproj_call(functools.partial(_oproj_split_kernel, lat_tiles=lat), H, mod, i, [yl, yc], [],
                               w_out, 0, plan.all_tiles, plan, name, y_specs=y_specs)

        if kind == 0:
            d = D // DIFF_HEADS // 2
            lam_init = 0.8 - 0.6 * math.exp(-0.3 * i)
            tabs = _rope_tables(plan, d, LANES)
            w_qkv = jnp.concatenate([_pair_halves(w_in[..., :D], d), _pair_halves(w_in[..., D:2 * D], d),
                                     w_in[..., 2 * D:]], axis=-1)
            (q, k, v), cast_out = project(functools.partial(_proj_diff_kernel, D=D, d=d), w_qkv,
                                          tabs, _table_specs(plan, tabs), [(D, BF16)] * 3, "proj_diff")
            small = [diff_lambda[j], diff_subln_g[j].reshape(1, 2 * d)]
            kern = functools.partial(_diff_attn_kernel, lam_init=lam_init, L=L, T=T)
            attend = functools.partial(_attn_call, kern, plan, q, k, v, lambda h: h, DIFF_HEADS, LANES, small,
                                       tq_lat=DIFF_TQ)
            yl = attend(context=False, name="diff_attn")
            yc = attend(context=True, name="diff_attn_ctx") if need_ctx else None
            H = out_proj(H, yl, yc, "oproj_diff")
        elif kind == 1:
            dk = D // RET_HEADS
            tabs = _rope_tables(plan, dk, dk)
            (q, k, v, g), cast_out = project(
                functools.partial(_proj_ret_kernel, D=D, dk=dk), w_in, tabs, _table_specs(plan, tabs),
                [(D, BF16), (D, BF16), (2 * D, BF16), (2 * D, BF16)], "proj_ret")
            yl, yc = _ret_call(plan, q, k, v, g, ret_decay_exp[j])
            H = out_proj(H, yl, yc if need_ctx else None, "oproj_ret")
        elif kind == 2:
            (q, v, g, zf, zb), cast_out = project(
                functools.partial(_proj_hgrn_kernel, D=D), w_in, [], [],
                [(D, BF16), (D, BF16), (D, BF16), (D, F32), (D, F32)], "proj_hgrn")
            o_f, o_b = _hgrn_call(plan, i, hgrn_lb_logits, q, v, zf, zb)
            H = _oproj_call(_oproj_hgrn_kernel, H, mod, i, [o_f, o_b, g],
                            [hgrn_norm_g[j].reshape(1, HGRN_HEAD_DIM)], w_out, 0,
                            mix_tiles, plan, "oproj_hgrn")
        else:
            d = GQA_HEAD_DIM
            kvw = GQA_KV_HEADS * d
            G = D // d // GQA_KV_HEADS
            tabs = _rope_tables(plan, d, LANES)
            extra = list(tabs) + [_pair_halves(gqa_q_norm_g[j].reshape(1, d), d),
                                  _pair_halves(gqa_k_norm_g[j].reshape(1, d), d)]
            specs = _table_specs(plan, tabs) + [pl.BlockSpec((1, d), lambda t: (0, 0))] * 2
            w_qkv = jnp.concatenate([_pair_halves(w_in[..., :D], d), _pair_halves(w_in[..., D:D + kvw], d),
                                     w_in[..., D + kvw:]], axis=-1)
            (q, k, v), cast_out = project(functools.partial(_proj_gqa_kernel, D=D, d=d, kvw=kvw), w_qkv,
                                          extra, specs, [(D, BF16), (kvw, BF16), (kvw, BF16)], "proj_gqa")
            kern = functools.partial(_gqa_attn_kernel, G=G, L=L, T=T)
            attend = functools.partial(_attn_call, kern, plan, q, k, v, lambda h: h, GQA_KV_HEADS, G * d, [],
                                       tq_lat=GQA_TQ)
            yl = attend(context=False, name="gqa_attn")
            yc = attend(context=True, name="gqa_attn_ctx") if need_ctx else None
            H = out_proj(H, yl, yc, "oproj_gqa")
        if cast_out:
            w13_cur, w2_cur = cast_out

        if need_ctx:
            nk, nj = (i + 1) % N_MIXERS, (i + 1) // N_MIXERS
            H, w13_cur, w2_cur, w_in, w_out = _ffn_call(
                H, mod, i, 6, w13_cur, w2_cur, 0, plan.all_tiles, plan,
                casts=[(ffn1_w13, i + 1), (ffn1_w2, i + 1), (mixer_w[nk][0], nj), (mixer_w[nk][1], nj)])
        else:
            (H,) = _ffn_call(H, mod, i, 6, w13_cur, w2_cur, 0, plan.lat_tiles, plan, final_g=final_norm_g)
    return H.reshape(B, T, D)
```

```python
import functools
import math

import jax
import jax.numpy as jnp
import numpy as np
from jax import lax
from jax.experimental import pallas as pl
from jax.experimental.pallas import tpu as pltpu

F32 = jnp.float32
BF16 = jnp.bfloat16

NORM_EPS = 1e-6
ROPE_THETA = 10000.0
GRID_W = 64
N_MOD = 9
N_MIXERS = 4
LOG2E = math.log2(math.e)

LANES = 128
MOD_ROWS = 8
VMEM_LIMIT = 56 * 1024 * 1024

DIFF_HEADS = 8
RET_HEADS = 4
GQA_KV_HEADS = 2
GQA_HEAD_DIM = 128
HGRN_HEAD_DIM = 128
FFN_CHUNK = 256
RET_CHUNK = 256
RET_UNROLL = 8
HGRN_CHUNK = 128
HGRN_BLOCK = 256
ATTN_KV_CHUNK = 256
DIFF_TQ = 2048
GQA_TQ = 512


def _params(*sem):
    return pltpu.CompilerParams(dimension_semantics=sem, vmem_limit_bytes=VMEM_LIMIT)


def _dot(a, b):
    return jnp.dot(a, b, preferred_element_type=F32)


def _dot_nt(a, b):
    return lax.dot_general(a, b, (((1,), (1,)), ((), ())), preferred_element_type=F32)


def _dot_tn(a, b):
    return lax.dot_general(a, b, (((0,), (0,)), ((), ())), preferred_element_type=F32)


def _silu(x):
    return x * jax.nn.sigmoid(x)


def _rms(x):
    return x * lax.rsqrt(jnp.mean(x * x, axis=-1, keepdims=True) + NORM_EPS)


def _rms_block(y):
    sq = y * y
    hi = sq.astype(BF16)
    lo = (sq - hi.astype(F32)).astype(BF16)
    ones = jnp.ones((2 * LANES, LANES), BF16)
    ms = _dot(jnp.concatenate([hi, lo], axis=1), ones) * (1.0 / LANES)
    return y * lax.rsqrt(ms + NORM_EPS)


def _modulated(h, mod_ref, k0):
    shift = mod_ref[k0:k0 + 1, :]
    scale = mod_ref[k0 + 1:k0 + 2, :]
    return _rms(h) * (1.0 + scale) + shift


HALF = LANES // 2


def _rope(y, c, s):
    return y * c + pltpu.roll(y, HALF, 1) * s


def _pair_halves(w, d):
    nq = d // 4
    lead = w.shape[:-1]
    x = w.reshape(*lead, -1, LANES // d, 2, 2, nq)
    return jnp.moveaxis(x, -2, -4).reshape(*lead, -1)


class _Plan:
    def __init__(self, B, T, L):
        self.B, self.T, self.L = B, T, L
        self.n_lat = B * T
        self.n_ctx = B * L
        self.NT = self.n_lat + self.n_ctx
        for tm in (1024, 512, 256):
            if T % tm == 0 and self.n_ctx % tm == 0:
                self.TM = tm
                break
        else:
            raise ValueError("unsupported sequence lengths")
        assert L == RET_CHUNK == HGRN_BLOCK and T % L == 0 and T % GRID_W == 0
        assert B + 1 <= MOD_ROWS
        self.tiles_per_batch = T // self.TM
        self.lat_tiles = self.n_lat // self.TM
        self.all_tiles = self.NT // self.TM

    def mod_row(self, i):
        return jnp.where(i < self.lat_tiles, i // self.tiles_per_batch, self.B)

    def pos_block(self, i):
        return jnp.where(i < self.lat_tiles, i % self.tiles_per_batch, self.tiles_per_batch)


def _rope_tables(plan, d, width):
    T = plan.T
    rows = T // GRID_W
    row = np.repeat(np.arange(rows, dtype=np.float32), GRID_W)
    col = np.tile(np.arange(GRID_W, dtype=np.float32), rows)
    nq = d // 4
    inv = (np.float32(ROPE_THETA) ** (-np.arange(nq, dtype=np.float32) * np.float32(2.0) / np.float32(d // 2)))
    inv = inv.astype(np.float32)
    ar, ac = row[:, None] * inv, col[:, None] * inv
    cr, sr, cc, sc = np.cos(ar), np.sin(ar), np.cos(ac), np.sin(ac)
    if 2 * nq == LANES:
        c = np.concatenate([cr, cr, cc, cc], axis=1)
        s = np.concatenate([-sr, sr, -sc, sc], axis=1)
    else:
        units = LANES // d
        c = np.concatenate([cr, cc] * (2 * units), axis=1)
        s = np.concatenate([-sr, -sc] * units + [sr, sc] * units, axis=1)
    assert c.shape[1] == width

    def finish(t, fill):
        t = np.concatenate([t, np.full((plan.TM, width), fill, np.float32)], axis=0)
        return jnp.asarray(t.astype(np.float32))

    return finish(c, 1.0), finish(s, 0.0)


def _mod_kernel(c_ref, w_ref, b_ref, o_ref):
    cond = _silu(c_ref[...]).astype(BF16)
    o_ref[...] = _dot(cond, w_ref[...].astype(BF16)) + b_ref[...]


def _mod_call(cstack, mod_w, mod_b):
    depth, D, _ = mod_w.shape
    out = pl.pallas_call(
        _mod_kernel,
        out_shape=jax.ShapeDtypeStruct((depth, MOD_ROWS, N_MOD * D), F32),
        grid=(depth, N_MOD),
        in_specs=[
            pl.BlockSpec((MOD_ROWS, D), lambda l, n: (0, 0)),
            pl.BlockSpec((None, D, D), lambda l, n: (l, 0, n)),
            pl.BlockSpec((None, 1, D), lambda l, n: (l, 0, n)),
        ],
        out_specs=pl.BlockSpec((None, MOD_ROWS, D), lambda l, n: (l, 0, n)),
        compiler_params=_params("arbitrary", "arbitrary"),
        name="mod_table",
    )(cstack, mod_w, mod_b.reshape(depth, 1, N_MOD * D))
    return out.reshape(depth, MOD_ROWS, N_MOD, D)


def _resident(shape, index_map):
    return pl.BlockSpec(shape, index_map, pipeline_mode=pl.Buffered(1))


BF16_ROWS = 16


def _cast_steps(n_tiles, rows):
    for c in range(n_tiles, 0, -1):
        if rows % (c * BF16_ROWS) == 0:
            return c
    raise ValueError("weights cannot be split into aligned slabs")


def _cast_plumbing(n_tiles, jobs):
    in_specs, args, out_shape, out_specs = [], [], [], []
    for arr, idx in jobs:
        _, R, C = arr.shape
        steps = _cast_steps(n_tiles, R)
        in_specs.append(pl.BlockSpec((None, R // steps, C),
                                     lambda i, idx=idx, steps=steps: (idx, jnp.minimum(i, steps - 1), 0)))
        args.append(arr)
        out_shape.append(jax.ShapeDtypeStruct((1, R, C), BF16))
        out_specs.append(pl.BlockSpec((None, R // steps, C),
                                      lambda i, steps=steps: (0, jnp.minimum(i, steps - 1), 0)))
    return in_specs, args, out_shape, out_specs


def _with_casts(kernel, n_in, n_out, n_jobs):
    if n_jobs == 0:
        return kernel

    def wrapped(*refs):
        ins, cast_in = refs[:n_in], refs[n_in:n_in + n_jobs]
        outs = refs[n_in + n_jobs:n_in + n_jobs + n_out]
        cast_out = refs[n_in + n_jobs + n_out:]
        for src, dst in zip(cast_in, cast_out):
            dst[...] = src[...].astype(BF16)
        kernel(*ins, *outs)

    return wrapped


def _ffn_kernel(*refs, k0, F, lat_tiles, split_in, final):
    refs = list(refs)
    if split_in:
        hl_ref, hc_ref = refs[:2]
        x = jnp.where(pl.program_id(0) < lat_tiles, hl_ref[...], hc_ref[...])
        refs = refs[2:]
    else:
        x = refs[0][...]
        refs = refs[1:]
    mod_ref, w13_ref, w2_ref = refs[:3]
    o_ref = refs[-1]
    xn = _modulated(x, mod_ref, k0).astype(BF16)
    acc = None
    for lo in range(0, F, FFN_CHUNK):
        a = _dot(xn, w13_ref[:, lo:lo + FFN_CHUNK])
        b = _dot(xn, w13_ref[:, F + lo:F + lo + FFN_CHUNK])
        part = _dot((_silu(a) * b).astype(BF16), w2_ref[lo:lo + FFN_CHUNK, :])
        acc = part if acc is None else acc + part
    hn = x + 0.5 * mod_ref[k0 + 2:k0 + 3, :] * acc
    if final:
        hn = _rms(hn) * refs[3][...]
    o_ref[...] = hn


def _ffn_call(h_in, mod, layer, k0, w13, w2, widx, n_tiles, plan, final_g=None, casts=()):
    split_in = isinstance(h_in, tuple)
    D, F = w2.shape[2], w2.shape[1]
    TM = plan.TM
    lat = plan.lat_tiles
    final = final_g is not None
    if split_in:
        in_specs = [pl.BlockSpec((TM, D), lambda i: (jnp.minimum(i, lat - 1), 0)),
                    pl.BlockSpec((TM, D), lambda i: (jnp.maximum(i - lat, 0), 0))]
        args = list(h_in)
    else:
        in_specs = [pl.BlockSpec((TM, D), lambda i: (i, 0))]
        args = [h_in]
    in_specs += [
        pl.BlockSpec((None, None, N_MOD, D), lambda i: (layer, plan.mod_row(i), 0, 0)),
        _resident((None, D, 2 * F), lambda i: (widx, 0, 0)),
        _resident((None, F, D), lambda i: (widx, 0, 0)),
    ]
    args += [mod, w13, w2]
    if final:
        in_specs.append(pl.BlockSpec((1, D), lambda i: (0, 0)))
        args.append(final_g.reshape(1, D))
    aliases = {} if (final or split_in) else {0: 0}
    rows = n_tiles * TM if final else plan.NT
    c_in, c_args, c_shape, c_out = _cast_plumbing(n_tiles, casts)
    kern = functools.partial(_ffn_kernel, k0=k0, F=F, lat_tiles=lat, split_in=split_in, final=final)
    return pl.pallas_call(
        _with_casts(kern, len(args), 1, len(casts)),
        out_shape=[jax.ShapeDtypeStruct((rows, D), F32)] + c_shape,
        grid=(n_tiles,),
        in_specs=in_specs + c_in,
        out_specs=[pl.BlockSpec((TM, D), lambda i: (i, 0))] + c_out,
        input_output_aliases=aliases,
        compiler_params=_params("arbitrary"),
        name="ffn_final" if final else "ffn",
    )(*args, *c_args)


PROJ_COLS = 256


def _proj_diff_kernel(h_ref, mod_ref, w_ref, c_ref, s_ref, q_ref, k_ref, v_ref, *, D, d):
    xm = _modulated(h_ref[...], mod_ref, 3).astype(BF16)
    c, s = c_ref[...], s_ref[...]
    for n in range(0, D, PROJ_COLS):
        yq = _dot(xm, w_ref[:, n:n + PROJ_COLS]) * (d ** -0.5 * LOG2E)
        yk = _dot(xm, w_ref[:, D + n:D + n + PROJ_COLS])
        for m in range(0, PROJ_COLS, LANES):
            q_ref[:, n + m:n + m + LANES] = _rope(yq[:, m:m + LANES], c, s).astype(BF16)
            k_ref[:, n + m:n + m + LANES] = _rope(yk[:, m:m + LANES], c, s).astype(BF16)
        v_ref[:, n:n + PROJ_COLS] = _dot(xm, w_ref[:, 2 * D + n:2 * D + n + PROJ_COLS]).astype(BF16)


def _proj_ret_kernel(h_ref, mod_ref, w_ref, c_ref, s_ref, q_ref, k_ref, v_ref, g_ref, *, D, dk):
    xm = _modulated(h_ref[...], mod_ref, 3).astype(BF16)
    c, s = c_ref[...], s_ref[...]
    for n in range(0, D, PROJ_COLS):
        yq = _dot(xm, w_ref[:, n:n + PROJ_COLS])
        yk = _dot(xm, w_ref[:, D + n:D + n + PROJ_COLS]) * (dk ** -0.5)
        for m in range(0, PROJ_COLS, LANES):
            cm, sm = c[:, m:m + LANES], s[:, m:m + LANES]
            q_ref[:, n + m:n + m + LANES] = _rope(yq[:, m:m + LANES], cm, sm).astype(BF16)
            k_ref[:, n + m:n + m + LANES] = _rope(yk[:, m:m + LANES], cm, sm).astype(BF16)
    for n in range(0, 2 * D, PROJ_COLS):
        v_ref[:, n:n + PROJ_COLS] = _dot(xm, w_ref[:, 2 * D + n:2 * D + n + PROJ_COLS]).astype(BF16)
        g_ref[:, n:n + PROJ_COLS] = _silu(_dot(xm, w_ref[:, 4 * D + n:4 * D + n + PROJ_COLS])).astype(BF16)


def _proj_hgrn_kernel(h_ref, mod_ref, w_ref, q_ref, i_ref, g_ref, zf_ref, zb_ref, *, D):
    xm = _modulated(h_ref[...], mod_ref, 3).astype(BF16)
    for n in range(0, D, PROJ_COLS):
        sl = slice(n, n + PROJ_COLS)
        q_ref[:, sl] = _silu(_dot(xm, w_ref[:, n:n + PROJ_COLS])).astype(BF16)
        i_ref[:, sl] = _dot(xm, w_ref[:, D + n:D + n + PROJ_COLS]).astype(BF16)
        g_ref[:, sl] = _silu(_dot(xm, w_ref[:, 2 * D + n:2 * D + n + PROJ_COLS])).astype(BF16)
        zf_ref[:, sl] = _dot(xm, w_ref[:, 3 * D + n:3 * D + n + PROJ_COLS])
        zb_ref[:, sl] = _dot(xm, w_ref[:, 4 * D + n:4 * D + n + PROJ_COLS])


def _proj_gqa_kernel(h_ref, mod_ref, w_ref, c_ref, s_ref, qg_ref, kg_ref,
                     q_ref, k_ref, v_ref, *, D, d, kvw):
    xm = _modulated(h_ref[...], mod_ref, 3).astype(BF16)
    c, s = c_ref[...], s_ref[...]
    qg = qg_ref[...] * (d ** -0.5 * LOG2E)
    kg = kg_ref[...]
    for n in range(0, D, PROJ_COLS):
        yq = _dot(xm, w_ref[:, n:n + PROJ_COLS])
        for m in range(0, PROJ_COLS, LANES):
            q_ref[:, n + m:n + m + LANES] = _rope(_rms_block(yq[:, m:m + LANES]) * qg, c, s).astype(BF16)
    yk = _dot(xm, w_ref[:, D:D + kvw])
    for m in range(0, kvw, LANES):
        k_ref[:, m:m + LANES] = _rope(_rms_block(yk[:, m:m + LANES]) * kg, c, s).astype(BF16)
    v_ref[...] = _dot(xm, w_ref[:, D + kvw:D + 2 * kvw]).astype(BF16)


def _proj_call(kernel, H, mod, layer, w, j, extra, extra_specs, outs, plan, name, casts=()):
    NT, D = H.shape
    TM = plan.TM
    in_specs = [
        pl.BlockSpec((TM, D), lambda i: (i, 0)),
        pl.BlockSpec((None, None, N_MOD, D), lambda i: (layer, plan.mod_row(i), 0, 0)),
        _resident((None,) + w.shape[1:], lambda i: (j, 0, 0)),
    ] + extra_specs
    args = [H, mod, w, *extra]
    c_in, c_args, c_shape, c_out = _cast_plumbing(plan.all_tiles, casts)
    return pl.pallas_call(
        _with_casts(kernel, len(args), len(outs), len(casts)),
        out_shape=[jax.ShapeDtypeStruct((NT, wd), dt) for wd, dt in outs] + c_shape,
        grid=(plan.all_tiles,),
        in_specs=in_specs + c_in,
        out_specs=[pl.BlockSpec((TM, wd), lambda i: (i, 0)) for wd, _ in outs] + c_out,
        compiler_params=_params("arbitrary"),
        name=name,
    )(*args, *c_args)


def _table_specs(plan, tables):
    return [pl.BlockSpec((plan.TM, t.shape[1]), lambda i: (plan.pos_block(i), 0)) for t in tables]


def _flash(q, sources):
    m = l = acc = None
    for k_ref, v_ref, start, size in sources:
        k = k_ref[start:start + size, :]
        v = v_ref[start:start + size, :]
        s = _dot_nt(q, k)
        tiles = [s[:, t:t + LANES] for t in range(0, size, LANES)]
        mx = functools.reduce(jnp.maximum, tiles)
        ms = jnp.broadcast_to(jnp.max(mx, axis=1, keepdims=True), mx.shape)
        if m is None:
            m = ms
            ps = [jnp.exp2(t - m) for t in tiles]
            l = functools.reduce(jnp.add, ps)
            acc = _dot(jnp.concatenate(ps, axis=1).astype(BF16), v)
        else:
            m_new = jnp.maximum(m, ms)
            alpha = jnp.exp2(m - m_new)
            ps = [jnp.exp2(t - m_new) for t in tiles]
            l = alpha * l + functools.reduce(jnp.add, ps)
            acc = alpha * acc + _dot(jnp.concatenate(ps, axis=1).astype(BF16), v)
            m = m_new
    return acc, jnp.sum(l, axis=1, keepdims=True)


def _kv_sources(kv_refs, L, T):
    if len(kv_refs) == 2:
        kc, vc = kv_refs
        return [(kc, vc, 0, L)]
    kc, vc, kl, vl = kv_refs
    chunk = min(ATTN_KV_CHUNK, T)
    return [(kc, vc, 0, L)] + [(kl, vl, s, chunk) for s in range(0, T, chunk)]


def _diff_attn_kernel(*refs, lam_init, L, T):
    lam_ref, g_ref, q_ref = refs[:3]
    kv_refs, o_ref = refs[3:-1], refs[-1]
    q = q_ref[...]
    tq = q.shape[0]
    lane = lax.broadcasted_iota(jnp.int32, q.shape, 1)
    map0 = (lane & (HALF // 2)) == 0
    zero = jnp.zeros_like(q)
    qs = jnp.concatenate([jnp.where(map0, q, zero), jnp.where(map0, zero, q)], axis=0)
    acc, l = _flash(qs, _kv_sources(kv_refs, L, T))
    lam = lam_ref[...]
    lam_full = (jnp.exp(jnp.sum(lam[0:1] * lam[1:2], axis=1, keepdims=True))
                - jnp.exp(jnp.sum(lam[2:3] * lam[3:4], axis=1, keepdims=True)) + lam_init)
    o = acc[:tq] / l[:tq] - lam_full * (acc[tq:] / l[tq:])
    o_ref[...] = (_rms(o) * g_ref[...] * (1.0 - lam_init)).astype(BF16)


def _gqa_attn_kernel(*refs, G, L, T):
    q_ref = refs[0]
    kv_refs, o_ref = refs[1:-1], refs[-1]
    tq = q_ref.shape[0]
    qs = jnp.concatenate([q_ref[:, g * LANES:(g + 1) * LANES] for g in range(G)], axis=0)
    acc, l = _flash(qs, _kv_sources(kv_refs, L, T))
    o = acc / l
    for g in range(G):
        o_ref[:, g * LANES:(g + 1) * LANES] = o[g * tq:(g + 1) * tq].astype(BF16)


def _attn_call(kernel, plan, q, k, v, kv_col, n_heads, q_width, small, *, tq_lat, context, name):
    B, T, L = plan.B, plan.T, plan.L
    ctx_row = plan.n_lat // L
    kvw = LANES
    small_specs = [pl.BlockSpec(s.shape, lambda b, h, i: (0, 0)) for s in small]
    kv_specs = [pl.BlockSpec((L, kvw), lambda b, h, i: (ctx_row + b, kv_col(h)))] * 2
    args = list(small) + [q, k, v]
    if context:
        tq, nq, rows = L, 1, plan.n_ctx
        q_spec = pl.BlockSpec((tq, q_width), lambda b, h, i: (ctx_row + b, h))
        o_spec = pl.BlockSpec((tq, q_width), lambda b, h, i: (b, h))
    else:
        tq = min(tq_lat, T)
        nq, rows = T // tq, plan.n_lat
        q_spec = o_spec = pl.BlockSpec((tq, q_width), lambda b, h, i: (b * nq + i, h))
        kv_specs = kv_specs + [pl.BlockSpec((T, kvw), lambda b, h, i: (b, kv_col(h)))] * 2
        args += [k, v]
    return pl.pallas_call(
        kernel,
        out_shape=jax.ShapeDtypeStruct((rows, q.shape[1]), BF16),
        grid=(B, n_heads, nq),
        in_specs=small_specs + [q_spec] + kv_specs,
        out_specs=o_spec,
        compiler_params=_params("arbitrary", "arbitrary", "arbitrary"),
        name=name,
    )(*args)


def _ret_kernel(dec_ref, qc_ref, kc_ref, vc_ref, gc_ref, ql_ref, kl_ref, vl_ref, gl_ref,
                yc_ref, yl_ref, sb_ref, st_ref, *, C, n_chunks):
    lg = jnp.log(1.0 - jnp.exp(-dec_ref[...] * math.log(2.0)))
    lgf, lgb = lg[0:1], lg[1:2]
    i = lax.broadcasted_iota(jnp.int32, (C, 1), 0).astype(F32)
    qdf, kef = jnp.exp((i + 1.0) * lgf), jnp.exp((C - 1.0 - i) * lgf)
    qdb, keb = jnp.exp((C - i) * lgb), jnp.exp(i * lgb)
    gfc, gbc = jnp.exp(C * lgf), jnp.exp(C * lgb)
    dist = (lax.broadcasted_iota(jnp.int32, (C, C), 0) - lax.broadcasted_iota(jnp.int32, (C, C), 1)).astype(F32)
    w = jnp.where(dist > 0, jnp.exp(jnp.maximum(dist, 0.0) * lgf),
                  jnp.where(dist < 0, jnp.exp(jnp.maximum(-dist, 0.0) * lgb), 2.0))

    def intra(q, k, v):
        return _dot((_dot_nt(q, k) * w).astype(BF16), v)

    def kv_state(k, v, ke):
        return _dot_tn((k.astype(F32) * ke).astype(BF16), v)

    def readout(o, g):
        return (g.astype(F32) * _rms(o)).astype(BF16)

    qx, kx, vx = qc_ref[...], kc_ref[...], vc_ref[...]
    yc_ref[...] = readout(intra(qx, kx, vx), gc_ref[...])

    st_ref[...] = kv_state(kx, vx, keb)

    def bwd(t, carry):
        c = n_chunks - 1 - t
        rows = pl.ds(pl.multiple_of(c * C, C), C)
        s = st_ref[...]
        sb_ref[c] = s.astype(BF16)
        st_ref[...] = gbc * s + kv_state(kl_ref[rows, :], vl_ref[rows, :], keb)
        return carry

    lax.fori_loop(0, n_chunks, bwd, 0, unroll=RET_UNROLL)

    st_ref[...] = kv_state(kx, vx, kef)

    def fwd(c, carry):
        rows = pl.ds(pl.multiple_of(c * C, C), C)
        q, k, v = ql_ref[rows, :], kl_ref[rows, :], vl_ref[rows, :]
        qf = q.astype(F32)
        s = st_ref[...]
        o = (intra(q, k, v) + _dot((qf * qdf).astype(BF16), s.astype(BF16))
             + _dot((qf * qdb).astype(BF16), sb_ref[c]))
        yl_ref[rows, :] = readout(o, gl_ref[rows, :])
        st_ref[...] = gfc * s + kv_state(k, v, kef)
        return carry

    lax.fori_loop(0, n_chunks, fwd, 0, unroll=RET_UNROLL)


def _ret_call(plan, q, k, v, g, decay_exp):
    B, T, L = plan.B, plan.T, plan.L
    NT, D = q.shape
    H = RET_HEADS
    dk, dv = D // H, v.shape[1] // H
    C = RET_CHUNK
    n_chunks = T // C
    ctx_row = plan.n_lat // L
    dec = jnp.transpose(decay_exp.astype(F32)).reshape(H, 2, 1)
    ctx = lambda w: pl.BlockSpec((L, w), lambda b, h: (ctx_row + b, h))
    lat = lambda w: pl.BlockSpec((T, w), lambda b, h: (b, h))
    yc, yl = pl.pallas_call(
        functools.partial(_ret_kernel, C=C, n_chunks=n_chunks),
        out_shape=[jax.ShapeDtypeStruct((plan.n_ctx, v.shape[1]), BF16),
                   jax.ShapeDtypeStruct((plan.n_lat, v.shape[1]), BF16)],
        grid=(B, H),
        in_specs=[pl.BlockSpec((None, 2, 1), lambda b, h: (h, 0, 0)),
                  ctx(dk), ctx(dk), ctx(dv), ctx(dv), lat(dk), lat(dk), lat(dv), lat(dv)],
        out_specs=[pl.BlockSpec((L, dv), lambda b, h: (b, h)),
                   pl.BlockSpec((T, dv), lambda b, h: (b, h))],
        scratch_shapes=[pltpu.VMEM((n_chunks, dk, dv), BF16), pltpu.VMEM((dk, dv), F32)],
        compiler_params=_params("arbitrary", "arbitrary"),
        name="retention",
    )(dec, q, k, v, g, q, k, v, g)
    return yl, yc


def _hgrn_lower_bound(logits, layer):
    e = jnp.exp(logits - jnp.max(logits, axis=0, keepdims=True))
    p = e / jnp.sum(e, axis=0, keepdims=True)
    return jnp.sum(p[0:layer + 1], axis=0, keepdims=True) - p[0:1]


def _hgrn_kernel(lb_ref, qf_ref, vf_ref, zf_ref, qb_ref, vb_ref, zb_ref, of_ref, ob_ref,
                 sf_ref, sb_ref, *, layer, C, n_heads):
    @pl.when(pl.program_id(1) == 0)
    def _():
        sf_ref[...] = jnp.zeros_like(sf_ref)
        sb_ref[...] = jnp.zeros_like(sb_ref)

    lb = _hgrn_lower_bound(lb_ref[...], layer)
    TB, D = qf_ref.shape
    n_sub = TB // C
    r_i = lax.broadcasted_iota(jnp.int32, (TB, TB), 0)
    c_i = lax.broadcasted_iota(jnp.int32, (TB, TB), 1)
    same_chunk = (r_i // C) == (c_i // C)
    m_r = lax.broadcasted_iota(jnp.int32, (C, C), 0)
    m_c = lax.broadcasted_iota(jnp.int32, (C, C), 1)

    def direction(q_ref, v_ref, z_ref, o_ref, st_ref, causal, mask, mid, last, order):
        f = lb + (1.0 - lb) * jax.nn.sigmoid(z_ref[...])
        kk = 1.0 - f
        la = jnp.log(f)
        tri = jnp.where(same_chunk & causal, 1.0, 0.0).astype(BF16)
        hi = la.astype(BF16)
        lo = (la - hi.astype(F32)).astype(BF16)
        b = _dot(tri, hi) + _dot(tri, lo)
        refs = [b[j * C + mid:j * C + mid + 1] for j in range(n_sub)]
        tots = [b[j * C + last:j * C + last + 1] for j in range(n_sub)]
        ref_rows = jnp.concatenate([jnp.broadcast_to(r, (C, D)) for r in refs], axis=0)
        dl = (b - ref_rows) * LOG2E
        qd = q_ref[...].astype(F32) * jnp.exp2(dl)
        kd = kk * jnp.exp2(-dl)
        v = v_ref[...]
        for j in order:
            rows = slice(j * C, (j + 1) * C)
            qd_j, kd_j = qd[rows], kd[rows]
            qdb, kdb = qd_j.astype(BF16), kd_j.astype(BF16)
            qs = (qd_j * jnp.exp(refs[j])).astype(BF16)
            ke = (kd_j * jnp.exp(tots[j] - refs[j])).astype(BF16)
            dec = jnp.exp(tots[j])
            for h in range(n_heads):
                sl = slice(h * HGRN_HEAD_DIM, (h + 1) * HGRN_HEAD_DIM)
                att = jnp.where(mask, _dot_nt(qdb[:, sl], kdb[:, sl]), 0.0).astype(BF16)
                st = st_ref[h]
                o = _dot(att, v[rows, sl]) + _dot_nt(qs[:, sl], st.astype(BF16))
                o_ref[rows, sl] = o.astype(o_ref.dtype)
                st_ref[h] = st * dec[:, sl] + _dot_tn(v[rows, sl], ke[:, sl])

    direction(qf_ref, vf_ref, zf_ref, of_ref, sf_ref, r_i >= c_i, m_r >= m_c, C // 2 - 1, C - 1,
              range(n_sub))
    direction(qb_ref, vb_ref, zb_ref, ob_ref, sb_ref, r_i <= c_i, m_r <= m_c, C // 2, 0,
              range(n_sub - 1, -1, -1))


def _hgrn_call(plan, layer, lb_logits, q, v, zf, zb):
    B, T, L = plan.B, plan.T, plan.L
    NT, D = q.shape
    TB = HGRN_BLOCK
    nb = T // TB
    ctx_blk = plan.n_lat // TB
    H = D // HGRN_HEAD_DIM
    fwd = lambda b, s: (jnp.where(s == 0, ctx_blk + b, b * nb + s - 1), 0)
    bwd = lambda b, s: (jnp.where(s == 0, ctx_blk + b, b * nb + nb - s), 0)
    blk = lambda im: pl.BlockSpec((TB, D), im)
    return pl.pallas_call(
        functools.partial(_hgrn_kernel, layer=layer, C=HGRN_CHUNK, n_heads=H),
        out_shape=[jax.ShapeDtypeStruct((NT, D), BF16), jax.ShapeDtypeStruct((NT, D), BF16)],
        grid=(B, nb + 1),
        in_specs=[pl.BlockSpec(lb_logits.shape, lambda b, s: (0, 0)),
                  blk(fwd), blk(fwd), blk(fwd), blk(bwd), blk(bwd), blk(bwd)],
        out_specs=[blk(fwd), blk(bwd)],
        scratch_shapes=[pltpu.VMEM((H, HGRN_HEAD_DIM, HGRN_HEAD_DIM), F32),
                        pltpu.VMEM((H, HGRN_HEAD_DIM, HGRN_HEAD_DIM), F32)],
        compiler_params=_params("arbitrary", "arbitrary"),
        name="hgrn2",
    )(lb_logits, q, v, zf, q, v, zb)


def _oproj_kernel(h_ref, mod_ref, y_ref, w_ref, o_ref):
    o_ref[...] = h_ref[...] + mod_ref[5:6, :] * _dot(y_ref[...], w_ref[...])


def _oproj_split_kernel(h_ref, mod_ref, yl_ref, yc_ref, w_ref, o_ref, *, lat_tiles):
    i = pl.program_id(0)

    @pl.when(i < lat_tiles)
    def _():
        o_ref[...] = h_ref[...] + mod_ref[5:6, :] * _dot(yl_ref[...], w_ref[...])

    @pl.when(i >= lat_tiles)
    def _():
        o_ref[...] = h_ref[...] + mod_ref[5:6, :] * _dot(yc_ref[...], w_ref[...])


def _oproj_hgrn_kernel(h_ref, mod_ref, of_ref, ob_ref, g_ref, ng_ref, w_ref, o_ref):
    o = of_ref[...].astype(F32) + ob_ref[...].astype(F32)
    ng = ng_ref[...]
    ys = []
    for n in range(0, o.shape[1], HGRN_HEAD_DIM):
        sl = slice(n, n + HGRN_HEAD_DIM)
        ys.append((_rms(o[:, sl]) * ng * g_ref[:, sl].astype(F32)).astype(BF16))
    y = jnp.concatenate(ys, axis=1)
    o_ref[...] = h_ref[...] + mod_ref[5:6, :] * _dot(y, w_ref[...])


def _oproj_call(kernel, H, mod, layer, ys, small, w, j, n_tiles, plan, name, y_specs=None):
    NT, D = H.shape
    TM = plan.TM
    if y_specs is None:
        y_specs = [pl.BlockSpec((TM, y.shape[1]), lambda i: (i, 0)) for y in ys]
    in_specs = ([pl.BlockSpec((TM, D), lambda i: (i, 0)),
                 pl.BlockSpec((None, None, N_MOD, D), lambda i: (layer, plan.mod_row(i), 0, 0))]
                + y_specs
                + [pl.BlockSpec(s.shape, lambda i: (0, 0)) for s in small]
                + [_resident((None,) + w.shape[1:], lambda i: (j, 0, 0))])
    return pl.pallas_call(
        kernel,
        out_shape=jax.ShapeDtypeStruct((NT, D), F32),
        grid=(n_tiles,),
        in_specs=in_specs,
        out_specs=pl.BlockSpec((TM, D), lambda i: (i, 0)),
        input_output_aliases={0: 0},
        compiler_params=_params("arbitrary"),
        name=name,
    )(H, mod, *ys, *small, w)


def kernel(x, c, ctx, c_ctx, mod_w, mod_b, ffn1_w13, ffn1_w2, ffn2_w13, ffn2_w2, diff_w_in, diff_w_out, diff_lambda, diff_subln_g, ret_w_in, ret_w_out, ret_decay_exp, hgrn_w_in, hgrn_w_out, hgrn_lb_logits, hgrn_norm_g, gqa_w_in, gqa_w_out, gqa_q_norm_g, gqa_k_norm_g, final_norm_g):
    B, T, D = x.shape
    L = ctx.shape[1]
    depth = mod_w.shape[0]
    plan = _Plan(B, T, L)

    cstack = jnp.concatenate([c, c_ctx[None, :], jnp.zeros((MOD_ROWS - B - 1, D), F32)], axis=0)
    mod = _mod_call(cstack, mod_w, mod_b)
    H = (x.reshape(B * T, D), ctx.reshape(B * L, D))
    bf = lambda w: w.astype(BF16)
    w13_cur, w2_cur = bf(ffn1_w13[0:1]), bf(ffn1_w2[0:1])
    mixer_w = [(diff_w_in, diff_w_out), (ret_w_in, ret_w_out), (hgrn_w_in, hgrn_w_out), (gqa_w_in, gqa_w_out)]
    w_in, w_out = bf(mixer_w[0][0][0:1]), bf(mixer_w[0][1][0:1])
    lat = plan.lat_tiles

    for i in range(depth):
        kind, j = i % N_MIXERS, i // N_MIXERS
        need_ctx = i < depth - 1
        mix_tiles = plan.all_tiles if need_ctx else plan.lat_tiles

        if i == 0:
            (H,) = _ffn_call(H, mod, i, 0, w13_cur, w2_cur, 0, plan.all_tiles, plan)
            proj_casts = [(ffn2_w13, i), (ffn2_w2, i)]
        else:
            H, w13_cur, w2_cur = _ffn_call(H, mod, i, 0, w13_cur, w2_cur, 0, plan.all_tiles, plan,
                                           casts=[(ffn2_w13, i), (ffn2_w2, i)])
            proj_casts = []

        def project(kern, w, extra, extra_specs, outs, name):
            res = _proj_call(kern, H, mod, i, w, 0, extra, extra_specs, outs, plan, name, casts=proj_casts)
            return res[:len(outs)], res[len(outs):]

        def out_proj(H, yl, yc, name):
            if yc is None:
                return _oproj_call(_oproj_kernel, H, mod, i, [yl], [], w_out, 0, lat, plan, name)
            width = yl.shape[1]
            y_specs = [pl.BlockSpec((plan.TM, width), lambda t: (jnp.minimum(t, lat - 1), 0)),
                       pl.BlockSpec((plan.TM, width), lambda t: (jnp.maximum(t - lat, 0), 0))]
            return _oproj_call(functools.partial(_oproj_split_kernel, lat_tiles=lat), H, mod, i, [yl, yc], [],
                               w_out, 0, plan.all_tiles, plan, name, y_specs=y_specs)

        if kind == 0:
            d = D // DIFF_HEADS // 2
            lam_init = 0.8 - 0.6 * math.exp(-0.3 * i)
            tabs = _rope_tables(plan, d, LANES)
            w_qkv = jnp.concatenate([_pair_halves(w_in[..., :D], d), _pair_halves(w_in[..., D:2 * D], d),
                                     w_in[..., 2 * D:]], axis=-1)
            (q, k, v), cast_out = project(functools.partial(_proj_diff_kernel, D=D, d=d), w_qkv,
                                          tabs, _table_specs(plan, tabs), [(D, BF16)] * 3, "proj_diff")
            small = [diff_lambda[j], diff_subln_g[j].reshape(1, 2 * d)]
            kern = functools.partial(_diff_attn_kernel, lam_init=lam_init, L=L, T=T)
            attend = functools.partial(_attn_call, kern, plan, q, k, v, lambda h: h, DIFF_HEADS, LANES, small,
                                       tq_lat=DIFF_TQ)
            yl = attend(context=False, name="diff_attn")
            yc = attend(context=True, name="diff_attn_ctx") if need_ctx else None
            H = out_proj(H, yl, yc, "oproj_diff")
        elif kind == 1:
            dk = D // RET_HEADS
            tabs = _rope_tables(plan, dk, dk)
            (q, k, v, g), cast_out = project(
                functools.partial(_proj_ret_kernel, D=D, dk=dk), w_in, tabs, _table_specs(plan, tabs),
                [(D, BF16), (D, BF16), (2 * D, BF16), (2 * D, BF16)], "proj_ret")
            yl, yc = _ret_call(plan, q, k, v, g, ret_decay_exp[j])
            H = out_proj(H, yl, yc if need_ctx else None, "oproj_ret")
        elif kind == 2:
            (q, v, g, zf, zb), cast_out = project(
                functools.partial(_proj_hgrn_kernel, D=D), w_in, [], [],
                [(D, BF16), (D, BF16), (D, BF16), (D, F32), (D, F32)], "proj_hgrn")
            o_f, o_b = _hgrn_call(plan, i, hgrn_lb_logits, q, v, zf, zb)
            H = _oproj_call(_oproj_hgrn_kernel, H, mod, i, [o_f, o_b, g],
                            [hgrn_norm_g[j].reshape(1, HGRN_HEAD_DIM)], w_out, 0,
                            mix_tiles, plan, "oproj_hgrn")
        else:
            d = GQA_HEAD_DIM
            kvw = GQA_KV_HEADS * d
            G = D // d // GQA_KV_HEADS
            tabs = _rope_tables(plan, d, LANES)
            extra = list(tabs) + [_pair_halves(gqa_q_norm_g[j].reshape(1, d), d),
                                  _pair_halves(gqa_k_norm_g[j].reshape(1, d), d)]
            specs = _table_specs(plan, tabs) + [pl.BlockSpec((1, d), lambda t: (0, 0))] * 2
            w_qkv = jnp.concatenate([_pair_halves(w_in[..., :D], d), _pair_halves(w_in[..., D:D + kvw], d),
                                     w_in[..., D + kvw:]], axis=-1)
            (q, k, v), cast_out = project(functools.partial(_proj_gqa_kernel, D=D, d=d, kvw=kvw), w_qkv,
                                          extra, specs, [(D, BF16), (kvw, BF16), (kvw, BF16)], "proj_gqa")
            kern = functools.partial(_gqa_attn_kernel, G=G, L=L, T=T)
            attend = functools.partial(_attn_call, kern, plan, q, k, v, lambda h: h, GQA_KV_HEADS, G * d, [],
                                       tq_lat=GQA_TQ)
            yl = attend(context=False, name="gqa_attn")
            yc = attend(context=True, name="gqa_attn_ctx") if need_ctx else None
            H = out_proj(H, yl, yc, "oproj_gqa")
        if cast_out:
            w13_cur, w2_cur = cast_out

        if need_ctx:
            nk, nj = (i + 1) % N_MIXERS, (i + 1) // N_MIXERS
            H, w13_cur, w2_cur, w_in, w_out = _ffn_call(
                H, mod, i, 6, w13_cur, w2_cur, 0, plan.all_tiles, plan,
                casts=[(ffn1_w13, i + 1), (ffn1_w2, i + 1), (mixer_w[nk][0], nj), (mixer_w[nk][1], nj)])
        else:
            (H,) = _ffn_call(H, mod, i, 6, w13_cur, w2_cur, 0, plan.lat_tiles, plan, final_g=final_norm_g)
    return H.reshape(B, T, D)
```

```python
import functools
import math

import jax
import jax.numpy as jnp
import numpy as np
from jax import lax
from jax.experimental import pallas as pl
from jax.experimental.pallas import tpu as pltpu

F32 = jnp.float32
BF16 = jnp.bfloat16

NORM_EPS = 1e-6
ROPE_THETA = 10000.0
GRID_W = 64
N_MOD = 9
N_MIXERS = 4
LOG2E = math.log2(math.e)

LANES = 128
MOD_ROWS = 8
VMEM_LIMIT = 56 * 1024 * 1024

DIFF_HEADS = 8
RET_HEADS = 4
GQA_KV_HEADS = 2
GQA_HEAD_DIM = 128
HGRN_HEAD_DIM = 128
FFN_CHUNK = 256
RET_CHUNK = 256
RET_UNROLL = 8
HGRN_CHUNK = 128
HGRN_BLOCK = 256
ATTN_KV_CHUNK = 256
DIFF_TQ = 2048
GQA_TQ = 512


def _params(*sem):
    return pltpu.CompilerParams(dimension_semantics=sem, vmem_limit_bytes=VMEM_LIMIT)


def _dot(a, b):
    return jnp.dot(a, b, preferred_element_type=F32)


def _dot_nt(a, b):
    return lax.dot_general(a, b, (((1,), (1,)), ((), ())), preferred_element_type=F32)


def _dot_tn(a, b):
    return lax.dot_general(a, b, (((0,), (0,)), ((), ())), preferred_element_type=F32)


def _silu(x):
    return x * jax.nn.sigmoid(x)


def _rms(x):
    return x * lax.rsqrt(jnp.mean(x * x, axis=-1, keepdims=True) + NORM_EPS)


def _rms_block(y):
    sq = y * y
    hi = sq.astype(BF16)
    lo = (sq - hi.astype(F32)).astype(BF16)
    ones = jnp.ones((2 * LANES, LANES), BF16)
    ms = _dot(jnp.concatenate([hi, lo], axis=1), ones) * (1.0 / LANES)
    return y * lax.rsqrt(ms + NORM_EPS)


def _modulated(h, mod_ref, k0):
    shift = mod_ref[k0:k0 + 1, :]
    scale = mod_ref[k0 + 1:k0 + 2, :]
    return _rms(h) * (1.0 + scale) + shift


HALF = LANES // 2


def _rope(y, c, s):
    return y * c + pltpu.roll(y, HALF, 1) * s


def _pair_halves(w, d):
    nq = d // 4
    lead = w.shape[:-1]
    x = w.reshape(*lead, -1, LANES // d, 2, 2, nq)
    return jnp.moveaxis(x, -2, -4).reshape(*lead, -1)


class _Plan:
    def __init__(self, B, T, L):
        self.B, self.T, self.L = B, T, L
        self.n_lat = B * T
        self.n_ctx = B * L
        self.NT = self.n_lat + self.n_ctx
        for tm in (1024, 512, 256):
            if T % tm == 0 and self.n_ctx % tm == 0:
                self.TM = tm
                break
        else:
            raise ValueError("unsupported sequence lengths")
        assert L == RET_CHUNK == HGRN_BLOCK and T % L == 0 and T % GRID_W == 0
        assert B + 1 <= MOD_ROWS
        self.tiles_per_batch = T // self.TM
        self.lat_tiles = self.n_lat // self.TM
        self.all_tiles = self.NT // self.TM

    def mod_row(self, i):
        return jnp.where(i < self.lat_tiles, i // self.tiles_per_batch, self.B)

    def pos_block(self, i):
        return jnp.where(i < self.lat_tiles, i % self.tiles_per_batch, self.tiles_per_batch)


def _rope_tables(plan, d, width):
    T = plan.T
    rows = T // GRID_W
    row = np.repeat(np.arange(rows, dtype=np.float32), GRID_W)
    col = np.tile(np.arange(GRID_W, dtype=np.float32), rows)
    nq = d // 4
    inv = (np.float32(ROPE_THETA) ** (-np.arange(nq, dtype=np.float32) * np.float32(2.0) / np.float32(d // 2)))
    inv = inv.astype(np.float32)
    ar, ac = row[:, None] * inv, col[:, None] * inv
    cr, sr, cc, sc = np.cos(ar), np.sin(ar), np.cos(ac), np.sin(ac)
    if 2 * nq == LANES:
        c = np.concatenate([cr, cr, cc, cc], axis=1)
        s = np.concatenate([-sr, sr, -sc, sc], axis=1)
    else:
        units = LANES // d
        c = np.concatenate([cr, cc] * (2 * units), axis=1)
        s = np.concatenate([-sr, -sc] * units + [sr, sc] * units, axis=1)
    assert c.shape[1] == width

    def finish(t, fill):
        t = np.concatenate([t, np.full((plan.TM, width), fill, np.float32)], axis=0)
        return jnp.asarray(t.astype(np.float32))

    return finish(c, 1.0), finish(s, 0.0)


def _mod_kernel(c_ref, w_ref, b_ref, o_ref):
    cond = _silu(c_ref[...]).astype(BF16)
    o_ref[...] = _dot(cond, w_ref[...].astype(BF16)) + b_ref[...]


def _mod_call(cstack, mod_w, mod_b):
    depth, D, _ = mod_w.shape
    out = pl.pallas_call(
        _mod_kernel,
        out_shape=jax.ShapeDtypeStruct((depth, MOD_ROWS, N_MOD * D), F32),
        grid=(depth, N_MOD),
        in_specs=[
            pl.BlockSpec((MOD_ROWS, D), lambda l, n: (0, 0)),
            pl.BlockSpec((None, D, D), lambda l, n: (l, 0, n)),
            pl.BlockSpec((None, 1, D), lambda l, n: (l, 0, n)),
        ],
        out_specs=pl.BlockSpec((None, MOD_ROWS, D), lambda l, n: (l, 0, n)),
        compiler_params=_params("arbitrary", "arbitrary"),
        name="mod_table",
    )(cstack, mod_w, mod_b.reshape(depth, 1, N_MOD * D))
    return out.reshape(depth, MOD_ROWS, N_MOD, D)


def _resident(shape, index_map):
    return pl.BlockSpec(shape, index_map, pipeline_mode=pl.Buffered(1))


BF16_ROWS = 16


def _cast_steps(n_tiles, rows):
    for c in range(n_tiles, 0, -1):
        if rows % (c * BF16_ROWS) == 0:
            return c
    raise ValueError("weights cannot be split into aligned slabs")


def _cast_plumbing(n_tiles, jobs):
    in_specs, args, out_shape, out_specs = [], [], [], []
    for arr, idx in jobs:
        _, R, C = arr.shape
        steps = _cast_steps(n_tiles, R)
        in_specs.append(pl.BlockSpec((None, R // steps, C),
                                     lambda i, idx=idx, steps=steps: (idx, jnp.minimum(i, steps - 1), 0)))
        args.append(arr)
        out_shape.append(jax.ShapeDtypeStruct((1, R, C), BF16))
        out_specs.append(pl.BlockSpec((None, R // steps, C),
                                      lambda i, steps=steps: (0, jnp.minimum(i, steps - 1), 0)))
    return in_specs, args, out_shape, out_specs


def _with_casts(kernel, n_in, n_out, n_jobs):
    if n_jobs == 0:
        return kernel

    def wrapped(*refs):
        ins, cast_in = refs[:n_in], refs[n_in:n_in + n_jobs]
        outs = refs[n_in + n_jobs:n_in + n_jobs + n_out]
        cast_out = refs[n_in + n_jobs + n_out:]
        for src, dst in zip(cast_in, cast_out):
            dst[...] = src[...].astype(BF16)
        kernel(*ins, *outs)

    return wrapped


def _ffn_kernel(*refs, k0, F, lat_tiles, split_in, final):
    refs = list(refs)
    if split_in:
        hl_ref, hc_ref = refs[:2]
        x = jnp.where(pl.program_id(0) < lat_tiles, hl_ref[...], hc_ref[...])
        refs = refs[2:]
    else:
        x = refs[0][...]
        refs = refs[1:]
    mod_ref, w13_ref, w2_ref = refs[:3]
    o_ref = refs[-1]
    xn = _modulated(x, mod_ref, k0).astype(BF16)
    acc = None
    for lo in range(0, F, FFN_CHUNK):
        a = _dot(xn, w13_ref[:, lo:lo + FFN_CHUNK])
        b = _dot(xn, w13_ref[:, F + lo:F + lo + FFN_CHUNK])
        part = _dot((_silu(a) * b).astype(BF16), w2_ref[lo:lo + FFN_CHUNK, :])
        acc = part if acc is None else acc + part
    hn = x + 0.5 * mod_ref[k0 + 2:k0 + 3, :] * acc
    if final:
        hn = _rms(hn) * refs[3][...]
    o_ref[...] = hn


def _ffn_call(h_in, mod, layer, k0, w13, w2, widx, n_tiles, plan, final_g=None, casts=()):
    split_in = isinstance(h_in, tuple)
    D, F = w2.shape[2], w2.shape[1]
    TM = plan.TM
    lat = plan.lat_tiles
    final = final_g is not None
    if split_in:
        in_specs = [pl.BlockSpec((TM, D), lambda i: (jnp.minimum(i, lat - 1), 0)),
                    pl.BlockSpec((TM, D), lambda i: (jnp.maximum(i - lat, 0), 0))]
        args = list(h_in)
    else:
        in_specs = [pl.BlockSpec((TM, D), lambda i: (i, 0))]
        args = [h_in]
    in_specs += [
        pl.BlockSpec((None, None, N_MOD, D), lambda i: (layer, plan.mod_row(i), 0, 0)),
        _resident((None, D, 2 * F), lambda i: (widx, 0, 0)),
        _resident((None, F, D), lambda i: (widx, 0, 0)),
    ]
    args += [mod, w13, w2]
    if final:
        in_specs.append(pl.BlockSpec((1, D), lambda i: (0, 0)))
        args.append(final_g.reshape(1, D))
    aliases = {} if (final or split_in) else {0: 0}
    rows = n_tiles * TM if final else plan.NT
    c_in, c_args, c_shape, c_out = _cast_plumbing(n_tiles, casts)
    kern = functools.partial(_ffn_kernel, k0=k0, F=F, lat_tiles=lat, split_in=split_in, final=final)
    return pl.pallas_call(
        _with_casts(kern, len(args), 1, len(casts)),
        out_shape=[jax.ShapeDtypeStruct((rows, D), F32)] + c_shape,
        grid=(n_tiles,),
        in_specs=in_specs + c_in,
        out_specs=[pl.BlockSpec((TM, D), lambda i: (i, 0))] + c_out,
        input_output_aliases=aliases,
        compiler_params=_params("arbitrary"),
        name="ffn_final" if final else "ffn",
    )(*args, *c_args)


PROJ_COLS = 256


def _proj_diff_kernel(h_ref, mod_ref, w_ref, c_ref, s_ref, q_ref, k_ref, v_ref, *, D, d):
    xm = _modulated(h_ref[...], mod_ref, 3).astype(BF16)
    c, s = c_ref[...], s_ref[...]
    for n in range(0, D, PROJ_COLS):
        yq = _dot(xm, w_ref[:, n:n + PROJ_COLS]) * (d ** -0.5 * LOG2E)
        yk = _dot(xm, w_ref[:, D + n:D + n + PROJ_COLS])
        for m in range(0, PROJ_COLS, LANES):
            q_ref[:, n + m:n + m + LANES] = _rope(yq[:, m:m + LANES], c, s).astype(BF16)
            k_ref[:, n + m:n + m + LANES] = _rope(yk[:, m:m + LANES], c, s).astype(BF16)
        v_ref[:, n:n + PROJ_COLS] = _dot(xm, w_ref[:, 2 * D + n:2 * D + n + PROJ_COLS]).astype(BF16)


def _proj_ret_kernel(h_ref, mod_ref, w_ref, c_ref, s_ref, q_ref, k_ref, v_ref, g_ref, *, D, dk):
    xm = _modulated(h_ref[...], mod_ref, 3).astype(BF16)
    c, s = c_ref[...], s_ref[...]
    for n in range(0, D, PROJ_COLS):
        yq = _dot(xm, w_ref[:, n:n + PROJ_COLS])
        yk = _dot(xm, w_ref[:, D + n:D + n + PROJ_COLS]) * (dk ** -0.5)
        for m in range(0, PROJ_COLS, LANES):
            cm, sm = c[:, m:m + LANES], s[:, m:m + LANES]
            q_ref[:, n + m:n + m + LANES] = _rope(yq[:, m:m + LANES], cm, sm).astype(BF16)
            k_ref[:, n + m:n + m + LANES] = _rope(yk[:, m:m + LANES], cm, sm).astype(BF16)
    for n in range(0, 2 * D, PROJ_COLS):
        v_ref[:, n:n + PROJ_COLS] = _dot(xm, w_ref[:, 2 * D + n:2 * D + n + PROJ_COLS]).astype(BF16)
        g_ref[:, n:n + PROJ_COLS] = _silu(_dot(xm, w_ref[:, 4 * D + n:4 * D + n + PROJ_COLS])).astype(BF16)


def _proj_hgrn_kernel(h_ref, mod_ref, w_ref, q_ref, i_ref, g_ref, zf_ref, zb_ref, *, D):
    xm = _modulated(h_ref[...], mod_ref, 3).astype(BF16)
    for n in range(0, D, PROJ_COLS):
        sl = slice(n, n + PROJ_COLS)
        q_ref[:, sl] = _silu(_dot(xm, w_ref[:, n:n + PROJ_COLS])).astype(BF16)
        i_ref[:, sl] = _dot(xm, w_ref[:, D + n:D + n + PROJ_COLS]).astype(BF16)
        g_ref[:, sl] = _silu(_dot(xm, w_ref[:, 2 * D + n:2 * D + n + PROJ_COLS])).astype(BF16)
        zf_ref[:, sl] = _dot(xm, w_ref[:, 3 * D + n:3 * D + n + PROJ_COLS])
        zb_ref[:, sl] = _dot(xm, w_ref[:, 4 * D + n:4 * D + n + PROJ_COLS])


def _proj_gqa_kernel(h_ref, mod_ref, w_ref, c_ref, s_ref, qg_ref, kg_ref,
                     q_ref, k_ref, v_ref, y_ref, *, D, d, kvw):
    @pl.when(pl.program_id(0) == 0)
    def _():
        y_ref[...] = jnp.zeros_like(y_ref)

    c, s = c_ref[...], s_ref[...]
    qg = qg_ref[...] * (d ** -0.5 * LOG2E)
    kg = kg_ref[...]
    for m in range(0, D, LANES):
        q_ref[:, m:m + LANES] = _rope(_rms_block(y_ref[:, m:m + LANES]) * qg, c, s).astype(BF16)
    for m in range(0, kvw, LANES):
        k_ref[:, m:m + LANES] = _rope(_rms_block(y_ref[:, D + m:D + m + LANES]) * kg, c, s).astype(BF16)
    v_ref[...] = y_ref[:, D + kvw:D + 2 * kvw].astype(BF16)

    xm = _modulated(h_ref[...], mod_ref, 3).astype(BF16)
    for n in range(0, D + 2 * kvw, PROJ_COLS):
        y_ref[:, n:n + PROJ_COLS] = _dot(xm, w_ref[:, n:n + PROJ_COLS])


def _proj_gqa_call(H, mod, layer, w, tabs, gains, plan, *, d, kvw):
    NT, D = H.shape
    TM = plan.TM
    last = plan.all_tiles - 1
    cur = lambda i: jnp.minimum(i, last)
    prev = lambda i: jnp.maximum(i - 1, 0)
    in_specs = [
        pl.BlockSpec((TM, D), lambda i: (cur(i), 0)),
        pl.BlockSpec((None, None, N_MOD, D), lambda i: (layer, plan.mod_row(cur(i)), 0, 0)),
        _resident((None,) + w.shape[1:], lambda i: (0, 0, 0)),
    ] + [pl.BlockSpec((TM, t.shape[1]), lambda i: (plan.pos_block(prev(i)), 0)) for t in tabs] \
      + [pl.BlockSpec((1, d), lambda i: (0, 0))] * 2
    widths = [D, kvw, kvw]
    return pl.pallas_call(
        functools.partial(_proj_gqa_kernel, D=D, d=d, kvw=kvw),
        out_shape=[jax.ShapeDtypeStruct((NT, wd), BF16) for wd in widths],
        grid=(plan.all_tiles + 1,),
        in_specs=in_specs,
        out_specs=[pl.BlockSpec((TM, wd), lambda i: (prev(i), 0)) for wd in widths],
        scratch_shapes=[pltpu.VMEM((TM, D + 2 * kvw), F32)],
        compiler_params=_params("arbitrary"),
        name="proj_gqa",
    )(H, mod, w, *tabs, *gains)


def _proj_call(kernel, H, mod, layer, w, j, extra, extra_specs, outs, plan, name, casts=()):
    NT, D = H.shape
    TM = plan.TM
    in_specs = [
        pl.BlockSpec((TM, D), lambda i: (i, 0)),
        pl.BlockSpec((None, None, N_MOD, D), lambda i: (layer, plan.mod_row(i), 0, 0)),
        _resident((None,) + w.shape[1:], lambda i: (j, 0, 0)),
    ] + extra_specs
    args = [H, mod, w, *extra]
    c_in, c_args, c_shape, c_out = _cast_plumbing(plan.all_tiles, casts)
    return pl.pallas_call(
        _with_casts(kernel, len(args), len(outs), len(casts)),
        out_shape=[jax.ShapeDtypeStruct((NT, wd), dt) for wd, dt in outs] + c_shape,
        grid=(plan.all_tiles,),
        in_specs=in_specs + c_in,
        out_specs=[pl.BlockSpec((TM, wd), lambda i: (i, 0)) for wd, _ in outs] + c_out,
        compiler_params=_params("arbitrary"),
        name=name,
    )(*args, *c_args)


def _table_specs(plan, tables):
    return [pl.BlockSpec((plan.TM, t.shape[1]), lambda i: (plan.pos_block(i), 0)) for t in tables]


def _flash(q, sources):
    m = l = acc = None
    for k_ref, v_ref, start, size in sources:
        k = k_ref[start:start + size, :]
        v = v_ref[start:start + size, :]
        s = _dot_nt(q, k)
        tiles = [s[:, t:t + LANES] for t in range(0, size, LANES)]
        mx = functools.reduce(jnp.maximum, tiles)
        ms = jnp.broadcast_to(jnp.max(mx, axis=1, keepdims=True), mx.shape)
        if m is None:
            m = ms
            ps = [jnp.exp2(t - m) for t in tiles]
            l = functools.reduce(jnp.add, ps)
            acc = _dot(jnp.concatenate(ps, axis=1).astype(BF16), v)
        else:
            m_new = jnp.maximum(m, ms)
            alpha = jnp.exp2(m - m_new)
            ps = [jnp.exp2(t - m_new) for t in tiles]
            l = alpha * l + functools.reduce(jnp.add, ps)
            acc = alpha * acc + _dot(jnp.concatenate(ps, axis=1).astype(BF16), v)
            m = m_new
    return acc, jnp.sum(l, axis=1, keepdims=True)


def _kv_sources(kv_refs, L, T):
    if len(kv_refs) == 2:
        kc, vc = kv_refs
        return [(kc, vc, 0, L)]
    kc, vc, kl, vl = kv_refs
    chunk = min(ATTN_KV_CHUNK, T)
    return [(kc, vc, 0, L)] + [(kl, vl, s, chunk) for s in range(0, T, chunk)]


def _diff_attn_kernel(*refs, lam_init, L, T):
    lam_ref, g_ref, q_ref = refs[:3]
    kv_refs, o_ref = refs[3:-1], refs[-1]
    q = q_ref[...]
    tq = q.shape[0]
    lane = lax.broadcasted_iota(jnp.int32, q.shape, 1)
    map0 = (lane & (HALF // 2)) == 0
    zero = jnp.zeros_like(q)
    qs = jnp.concatenate([jnp.where(map0, q, zero), jnp.where(map0, zero, q)], axis=0)
    acc, l = _flash(qs, _kv_sources(kv_refs, L, T))
    lam = lam_ref[...]
    lam_full = (jnp.exp(jnp.sum(lam[0:1] * lam[1:2], axis=1, keepdims=True))
                - jnp.exp(jnp.sum(lam[2:3] * lam[3:4], axis=1, keepdims=True)) + lam_init)
    o = acc[:tq] / l[:tq] - lam_full * (acc[tq:] / l[tq:])
    o_ref[...] = (_rms(o) * g_ref[...] * (1.0 - lam_init)).astype(BF16)


def _gqa_attn_kernel(*refs, G, L, T):
    q_ref = refs[0]
    kv_refs, o_ref = refs[1:-1], refs[-1]
    tq = q_ref.shape[0]
    qs = jnp.concatenate([q_ref[:, g * LANES:(g + 1) * LANES] for g in range(G)], axis=0)
    acc, l = _flash(qs, _kv_sources(kv_refs, L, T))
    o = acc / l
    for g in range(G):
        o_ref[:, g * LANES:(g + 1) * LANES] = o[g * tq:(g + 1) * tq].astype(BF16)


def _attn_call(kernel, plan, q, k, v, kv_col, n_heads, q_width, small, *, tq_lat, context, name):
    B, T, L = plan.B, plan.T, plan.L
    ctx_row = plan.n_lat // L
    kvw = LANES
    small_specs = [pl.BlockSpec(s.shape, lambda b, h, i: (0, 0)) for s in small]
    kv_specs = [pl.BlockSpec((L, kvw), lambda b, h, i: (ctx_row + b, kv_col(h)))] * 2
    args = list(small) + [q, k, v]
    if context:
        tq, nq, rows = L, 1, plan.n_ctx
        q_spec = pl.BlockSpec((tq, q_width), lambda b, h, i: (ctx_row + b, h))
        o_spec = pl.BlockSpec((tq, q_width), lambda b, h, i: (b, h))
    else:
        tq = min(tq_lat, T)
        nq, rows = T // tq, plan.n_lat
        q_spec = o_spec = pl.BlockSpec((tq, q_width), lambda b, h, i: (b * nq + i, h))
        kv_specs = kv_specs + [pl.BlockSpec((T, kvw), lambda b, h, i: (b, kv_col(h)))] * 2
        args += [k, v]
    return pl.pallas_call(
        kernel,
        out_shape=jax.ShapeDtypeStruct((rows, q.shape[1]), BF16),
        grid=(B, n_heads, nq),
        in_specs=small_specs + [q_spec] + kv_specs,
        out_specs=o_spec,
        compiler_params=_params("arbitrary", "arbitrary", "arbitrary"),
        name=name,
    )(*args)


def _ret_kernel(dec_ref, qc_ref, kc_ref, vc_ref, gc_ref, ql_ref, kl_ref, vl_ref, gl_ref,
                yc_ref, yl_ref, sb_ref, st_ref, *, C, n_chunks):
    lg = jnp.log(1.0 - jnp.exp(-dec_ref[...] * math.log(2.0)))
    lgf, lgb = lg[0:1], lg[1:2]
    i = lax.broadcasted_iota(jnp.int32, (C, 1), 0).astype(F32)
    qdf, kef = jnp.exp((i + 1.0) * lgf), jnp.exp((C - 1.0 - i) * lgf)
    qdb, keb = jnp.exp((C - i) * lgb), jnp.exp(i * lgb)
    gfc, gbc = jnp.exp(C * lgf), jnp.exp(C * lgb)
    dist = (lax.broadcasted_iota(jnp.int32, (C, C), 0) - lax.broadcasted_iota(jnp.int32, (C, C), 1)).astype(F32)
    w = jnp.where(dist > 0, jnp.exp(jnp.maximum(dist, 0.0) * lgf),
                  jnp.where(dist < 0, jnp.exp(jnp.maximum(-dist, 0.0) * lgb), 2.0))

    def intra(q, k, v):
        return _dot((_dot_nt(q, k) * w).astype(BF16), v)

    def kv_state(k, v, ke):
        return _dot_tn((k.astype(F32) * ke).astype(BF16), v)

    def readout(o, g):
        return (g.astype(F32) * _rms(o)).astype(BF16)

    qx, kx, vx = qc_ref[...], kc_ref[...], vc_ref[...]
    yc_ref[...] = readout(intra(qx, kx, vx), gc_ref[...])

    st_ref[...] = kv_state(kx, vx, keb)

    def bwd(t, carry):
        c = n_chunks - 1 - t
        rows = pl.ds(pl.multiple_of(c * C, C), C)
        s = st_ref[...]
        sb_ref[c] = s.astype(BF16)
        st_ref[...] = gbc * s + kv_state(kl_ref[rows, :], vl_ref[rows, :], keb)
        return carry

    lax.fori_loop(0, n_chunks, bwd, 0, unroll=RET_UNROLL)

    st_ref[...] = kv_state(kx, vx, kef)

    def fwd(c, carry):
        rows = pl.ds(pl.multiple_of(c * C, C), C)
        q, k, v = ql_ref[rows, :], kl_ref[rows, :], vl_ref[rows, :]
        qf = q.astype(F32)
        s = st_ref[...]
        o = (intra(q, k, v) + _dot((qf * qdf).astype(BF16), s.astype(BF16))
             + _dot((qf * qdb).astype(BF16), sb_ref[c]))
        yl_ref[rows, :] = readout(o, gl_ref[rows, :])
        st_ref[...] = gfc * s + kv_state(k, v, kef)
        return carry

    lax.fori_loop(0, n_chunks, fwd, 0, unroll=RET_UNROLL)


def _ret_call(plan, q, k, v, g, decay_exp):
    B, T, L = plan.B, plan.T, plan.L
    NT, D = q.shape
    H = RET_HEADS
    dk, dv = D // H, v.shape[1] // H
    C = RET_CHUNK
    n_chunks = T // C
    ctx_row = plan.n_lat // L
    dec = jnp.transpose(decay_exp.astype(F32)).reshape(H, 2, 1)
    ctx = lambda w: pl.BlockSpec((L, w), lambda b, h: (ctx_row + b, h))
    lat = lambda w: pl.BlockSpec((T, w), lambda b, h: (b, h))
    yc, yl = pl.pallas_call(
        functools.partial(_ret_kernel, C=C, n_chunks=n_chunks),
        out_shape=[jax.ShapeDtypeStruct((plan.n_ctx, v.shape[1]), BF16),
                   jax.ShapeDtypeStruct((plan.n_lat, v.shape[1]), BF16)],
        grid=(B, H),
        in_specs=[pl.BlockSpec((None, 2, 1), lambda b, h: (h, 0, 0)),
                  ctx(dk), ctx(dk), ctx(dv), ctx(dv), lat(dk), lat(dk), lat(dv), lat(dv)],
        out_specs=[pl.BlockSpec((L, dv), lambda b, h: (b, h)),
                   pl.BlockSpec((T, dv), lambda b, h: (b, h))],
        scratch_shapes=[pltpu.VMEM((n_chunks, dk, dv), BF16), pltpu.VMEM((dk, dv), F32)],
        compiler_params=_params("arbitrary", "arbitrary"),
        name="retention",
    )(dec, q, k, v, g, q, k, v, g)
    return yl, yc


def _hgrn_lower_bound(logits, layer):
    e = jnp.exp(logits - jnp.max(logits, axis=0, keepdims=True))
    p = e / jnp.sum(e, axis=0, keepdims=True)
    return jnp.sum(p[0:layer + 1], axis=0, keepdims=True) - p[0:1]


def _hgrn_kernel(lb_ref, qf_ref, vf_ref, zf_ref, qb_ref, vb_ref, zb_ref, of_ref, ob_ref,
                 sf_ref, sb_ref, *, layer, C, n_heads):
    @pl.when(pl.program_id(1) == 0)
    def _():
        sf_ref[...] = jnp.zeros_like(sf_ref)
        sb_ref[...] = jnp.zeros_like(sb_ref)

    lb = _hgrn_lower_bound(lb_ref[...], layer)
    TB, D = qf_ref.shape
    n_sub = TB // C
    r_i = lax.broadcasted_iota(jnp.int32, (TB, TB), 0)
    c_i = lax.broadcasted_iota(jnp.int32, (TB, TB), 1)
    same_chunk = (r_i // C) == (c_i // C)
    m_r = lax.broadcasted_iota(jnp.int32, (C, C), 0)
    m_c = lax.broadcasted_iota(jnp.int32, (C, C), 1)

    def direction(q_ref, v_ref, z_ref, o_ref, st_ref, causal, mask, mid, last, order):
        f = lb + (1.0 - lb) * jax.nn.sigmoid(z_ref[...])
        kk = 1.0 - f
        la = jnp.log(f)
        tri = jnp.where(same_chunk & causal, 1.0, 0.0).astype(BF16)
        hi = la.astype(BF16)
        lo = (la - hi.astype(F32)).astype(BF16)
        b = _dot(tri, hi) + _dot(tri, lo)
        refs = [b[j * C + mid:j * C + mid + 1] for j in range(n_sub)]
        tots = [b[j * C + last:j * C + last + 1] for j in range(n_sub)]
        ref_rows = jnp.concatenate([jnp.broadcast_to(r, (C, D)) for r in refs], axis=0)
        dl = (b - ref_rows) * LOG2E
        qd = q_ref[...].astype(F32) * jnp.exp2(dl)
        kd = kk * jnp.exp2(-dl)
        v = v_ref[...]
        for j in order:
            rows = slice(j * C, (j + 1) * C)
            qd_j, kd_j = qd[rows], kd[rows]
            qdb, kdb = qd_j.astype(BF16), kd_j.astype(BF16)
            qs = (qd_j * jnp.exp(refs[j])).astype(BF16)
            ke = (kd_j * jnp.exp(tots[j] - refs[j])).astype(BF16)
            dec = jnp.exp(tots[j])
            for h in range(n_heads):
                sl = slice(h * HGRN_HEAD_DIM, (h + 1) * HGRN_HEAD_DIM)
                att = jnp.where(mask, _dot_nt(qdb[:, sl], kdb[:, sl]), 0.0).astype(BF16)
                st = st_ref[h]
                o = _dot(att, v[rows, sl]) + _dot_nt(qs[:, sl], st.astype(BF16))
                o_ref[rows, sl] = o.astype(o_ref.dtype)
                st_ref[h] = st * dec[:, sl] + _dot_tn(v[rows, sl], ke[:, sl])

    direction(qf_ref, vf_ref, zf_ref, of_ref, sf_ref, r_i >= c_i, m_r >= m_c, C // 2 - 1, C - 1,
              range(n_sub))
    direction(qb_ref, vb_ref, zb_ref, ob_ref, sb_ref, r_i <= c_i, m_r <= m_c, C // 2, 0,
              range(n_sub - 1, -1, -1))


def _hgrn_call(plan, layer, lb_logits, q, v, zf, zb):
    B, T, L = plan.B, plan.T, plan.L
    NT, D = q.shape
    TB = HGRN_BLOCK
    nb = T // TB
    ctx_blk = plan.n_lat // TB
    H = D // HGRN_HEAD_DIM
    fwd = lambda b, s: (jnp.where(s == 0, ctx_blk + b, b * nb + s - 1), 0)
    bwd = lambda b, s: (jnp.where(s == 0, ctx_blk + b, b * nb + nb - s), 0)
    blk = lambda im: pl.BlockSpec((TB, D), im)
    return pl.pallas_call(
        functools.partial(_hgrn_kernel, layer=layer, C=HGRN_CHUNK, n_heads=H),
        out_shape=[jax.ShapeDtypeStruct((NT, D), BF16), jax.ShapeDtypeStruct((NT, D), BF16)],
        grid=(B, nb + 1),
        in_specs=[pl.BlockSpec(lb_logits.shape, lambda b, s: (0, 0)),
                  blk(fwd), blk(fwd), blk(fwd), blk(bwd), blk(bwd), blk(bwd)],
        out_specs=[blk(fwd), blk(bwd)],
        scratch_shapes=[pltpu.VMEM((H, HGRN_HEAD_DIM, HGRN_HEAD_DIM), F32),
                        pltpu.VMEM((H, HGRN_HEAD_DIM, HGRN_HEAD_DIM), F32)],
        compiler_params=_params("arbitrary", "arbitrary"),
        name="hgrn2",
    )(lb_logits, q, v, zf, q, v, zb)


def _oproj_kernel(h_ref, mod_ref, y_ref, w_ref, o_ref):
    o_ref[...] = h_ref[...] + mod_ref[5:6, :] * _dot(y_ref[...], w_ref[...])


def _oproj_split_kernel(h_ref, mod_ref, yl_ref, yc_ref, w_ref, o_ref, *, lat_tiles):
    i = pl.program_id(0)

    @pl.when(i < lat_tiles)
    def _():
        o_ref[...] = h_ref[...] + mod_ref[5:6, :] * _dot(yl_ref[...], w_ref[...])

    @pl.when(i >= lat_tiles)
    def _():
        o_ref[...] = h_ref[...] + mod_ref[5:6, :] * _dot(yc_ref[...], w_ref[...])


def _oproj_hgrn_kernel(h_ref, mod_ref, of_ref, ob_ref, g_ref, ng_ref, w_ref, o_ref):
    o = of_ref[...].astype(F32) + ob_ref[...].astype(F32)
    ng = ng_ref[...]
    ys = []
    for n in range(0, o.shape[1], HGRN_HEAD_DIM):
        sl = slice(n, n + HGRN_HEAD_DIM)
        ys.append((_rms(o[:, sl]) * ng * g_ref[:, sl].astype(F32)).astype(BF16))
    y = jnp.concatenate(ys, axis=1)
    o_ref[...] = h_ref[...] + mod_ref[5:6, :] * _dot(y, w_ref[...])


def _oproj_call(kernel, H, mod, layer, ys, small, w, j, n_tiles, plan, name, y_specs=None):
    NT, D = H.shape
    TM = plan.TM
    if y_specs is None:
        y_specs = [pl.BlockSpec((TM, y.shape[1]), lambda i: (i, 0)) for y in ys]
    in_specs = ([pl.BlockSpec((TM, D), lambda i: (i, 0)),
                 pl.BlockSpec((None, None, N_MOD, D), lambda i: (layer, plan.mod_row(i), 0, 0))]
                + y_specs
                + [pl.BlockSpec(s.shape, lambda i: (0, 0)) for s in small]
                + [_resident((None,) + w.shape[1:], lambda i: (j, 0, 0))])
    return pl.pallas_call(
        kernel,
        out_shape=jax.ShapeDtypeStruct((NT, D), F32),
        grid=(n_tiles,),
        in_specs=in_specs,
        out_specs=pl.BlockSpec((TM, D), lambda i: (i, 0)),
        input_output_aliases={0: 0},
        compiler_params=_params("arbitrary"),
        name=name,
    )(H, mod, *ys, *small, w)


def kernel(x, c, ctx, c_ctx, mod_w, mod_b, ffn1_w13, ffn1_w2, ffn2_w13, ffn2_w2, diff_w_in, diff_w_out, diff_lambda, diff_subln_g, ret_w_in, ret_w_out, ret_decay_exp, hgrn_w_in, hgrn_w_out, hgrn_lb_logits, hgrn_norm_g, gqa_w_in, gqa_w_out, gqa_q_norm_g, gqa_k_norm_g, final_norm_g):
    B, T, D = x.shape
    L = ctx.shape[1]
    depth = mod_w.shape[0]
    plan = _Plan(B, T, L)

    cstack = jnp.concatenate([c, c_ctx[None, :], jnp.zeros((MOD_ROWS - B - 1, D), F32)], axis=0)
    mod = _mod_call(cstack, mod_w, mod_b)
    H = (x.reshape(B * T, D), ctx.reshape(B * L, D))
    bf = lambda w: w.astype(BF16)
    w13_cur, w2_cur = bf(ffn1_w13[0:1]), bf(ffn1_w2[0:1])
    mixer_w = [(diff_w_in, diff_w_out), (ret_w_in, ret_w_out), (hgrn_w_in, hgrn_w_out), (gqa_w_in, gqa_w_out)]
    w_in, w_out = bf(mixer_w[0][0][0:1]), bf(mixer_w[0][1][0:1])
    lat = plan.lat_tiles

    for i in range(depth):
        kind, j = i % N_MIXERS, i // N_MIXERS
        need_ctx = i < depth - 1
        mix_tiles = plan.all_tiles if need_ctx else plan.lat_tiles

        if i == 0:
            (H,) = _ffn_call(H, mod, i, 0, w13_cur, w2_cur, 0, plan.all_tiles, plan)
            proj_casts = [(ffn2_w13, i), (ffn2_w2, i)]
        else:
            H, w13_cur, w2_cur = _ffn_call(H, mod, i, 0, w13_cur, w2_cur, 0, plan.all_tiles, plan,
                                           casts=[(ffn2_w13, i), (ffn2_w2, i)])
            proj_casts = []

        def project(kern, w, extra, extra_specs, outs, name):
            res = _proj_call(kern, H, mod, i, w, 0, extra, extra_specs, outs, plan, name, casts=proj_casts)
            return res[:len(outs)], res[len(outs):]

        def out_proj(H, yl, yc, name):
            if yc is None:
                return _oproj_call(_oproj_kernel, H, mod, i, [yl], [], w_out, 0, lat, plan, name)
            width = yl.shape[1]
            y_specs = [pl.BlockSpec((plan.TM, width), lambda t: (jnp.minimum(t, lat - 1), 0)),
                       pl.BlockSpec((plan.TM, width), lambda t: (jnp.maximum(t - lat, 0), 0))]
            return _oproj_call(functools.partial(_oproj_split_kernel, lat_tiles=lat), H, mod, i, [yl, yc], [],
                               w_out, 0, plan.all_tiles, plan, name, y_specs=y_specs)

        if kind == 0:
            d = D // DIFF_HEADS // 2
            lam_init = 0.8 - 0.6 * math.exp(-0.3 * i)
            tabs = _rope_tables(plan, d, LANES)
            w_qkv = jnp.concatenate([_pair_halves(w_in[..., :D], d), _pair_halves(w_in[..., D:2 * D], d),
                                     w_in[..., 2 * D:]], axis=-1)
            (q, k, v), cast_out = project(functools.partial(_proj_diff_kernel, D=D, d=d), w_qkv,
                                          tabs, _table_specs(plan, tabs), [(D, BF16)] * 3, "proj_diff")
            small = [diff_lambda[j], diff_subln_g[j].reshape(1, 2 * d)]
            kern = functools.partial(_diff_attn_kernel, lam_init=lam_init, L=L, T=T)
            attend = functools.partial(_attn_call, kern, plan, q, k, v, lambda h: h, DIFF_HEADS, LANES, small,
                                       tq_lat=DIFF_TQ)
            yl = attend(context=False, name="diff_attn")
            yc = attend(context=True, name="diff_attn_ctx") if need_ctx else None
            H = out_proj(H, yl, yc, "oproj_diff")
        elif kind == 1:
            dk = D // RET_HEADS
            tabs = _rope_tables(plan, dk, dk)
            (q, k, v, g), cast_out = project(
                functools.partial(_proj_ret_kernel, D=D, dk=dk), w_in, tabs, _table_specs(plan, tabs),
                [(D, BF16), (D, BF16), (2 * D, BF16), (2 * D, BF16)], "proj_ret")
            yl, yc = _ret_call(plan, q, k, v, g, ret_decay_exp[j])
            H = out_proj(H, yl, yc if need_ctx else None, "oproj_ret")
        elif kind == 2:
            (q, v, g, zf, zb), cast_out = project(
                functools.partial(_proj_hgrn_kernel, D=D), w_in, [], [],
                [(D, BF16), (D, BF16), (D, BF16), (D, F32), (D, F32)], "proj_hgrn")
            o_f, o_b = _hgrn_call(plan, i, hgrn_lb_logits, q, v, zf, zb)
            H = _oproj_call(_oproj_hgrn_kernel, H, mod, i, [o_f, o_b, g],
                            [hgrn_norm_g[j].reshape(1, HGRN_HEAD_DIM)], w_out, 0,
                            mix_tiles, plan, "oproj_hgrn")
        else:
            d = GQA_HEAD_DIM
            kvw = GQA_KV_HEADS * d
            G = D // d // GQA_KV_HEADS
            tabs = _rope_tables(plan, d, LANES)
            extra = list(tabs) + [_pair_halves(gqa_q_norm_g[j].reshape(1, d), d),
                                  _pair_halves(gqa_k_norm_g[j].reshape(1, d), d)]
            w_qkv = jnp.concatenate([_pair_halves(w_in[..., :D], d), _pair_halves(w_in[..., D:D + kvw], d),
                                     w_in[..., D + kvw:]], axis=-1)
            assert not proj_casts, "the skewed GQA projection hosts no cast jobs"
            q, k, v = _proj_gqa_call(H, mod, i, w_qkv, tabs, extra[len(tabs):], plan, d=d, kvw=kvw)
            cast_out = []
            kern = functools.partial(_gqa_attn_kernel, G=G, L=L, T=T)
            attend = functools.partial(_attn_call, kern, plan, q, k, v, lambda h: h, GQA_KV_HEADS, G * d, [],
                                       tq_lat=GQA_TQ)
            yl = attend(context=False, name="gqa_attn")
            yc = attend(context=True, name="gqa_attn_ctx") if need_ctx else None
            H = out_proj(H, yl, yc, "oproj_gqa")
        if cast_out:
            w13_cur, w2_cur = cast_out

        if need_ctx:
            nk, nj = (i + 1) % N_MIXERS, (i + 1) // N_MIXERS
            H, w13_cur, w2_cur, w_in, w_out = _ffn_call(
                H, mod, i, 6, w13_cur, w2_cur, 0, plan.all_tiles, plan,
                casts=[(ffn1_w13, i + 1), (ffn1_w2, i + 1), (mixer_w[nk][0], nj), (mixer_w[nk][1], nj)])
        else:
            (H,) = _ffn_call(H, mod, i, 6, w13_cur, w2_cur, 0, plan.lat_tiles, plan, final_g=final_norm_g)
    return H.reshape(B, T, D)
```
